```python
import jax
import jax.numpy as jnp
from jax import lax
import numpy as np

D_MODEL = 2048
BATCH = 2
SEQ = 4096
DEPTH = 1

GRID_W = 64
HEAD_DIM = 128
N_Q_HEADS = 8
N_KV_HEADS = 2
Q_PER_KV = N_Q_HEADS // N_KV_HEADS
ATTN_WIDTH = N_Q_HEADS * HEAD_DIM
KV_WIDTH = N_KV_HEADS * HEAD_DIM
Q_BLOCK = 128
ROPE_THETA = 10000.0
POOL_WINDOWS = (2, 4, 8, 16)
N_POOL_GROUPS = len(POOL_WINDOWS)
POOL_WIDTH = D_MODEL // 2
POOL_GROUP = POOL_WIDTH // N_POOL_GROUPS
N_BRANCHES = 2
Q_END = ATTN_WIDTH
K_END = Q_END + KV_WIDTH
V_END = K_END + KV_WIDTH
P_END = V_END + POOL_WIDTH
GA_END = P_END + D_MODEL
IN_WIDTH = GA_END + D_MODEL
MEM_LEN = 256
N_MEM_HEADS = 4
MEM_HEAD_DIM = 128
MEM_WIDTH = N_MEM_HEADS * MEM_HEAD_DIM
N_EXPERTS = 32
TOP_K = 4
D_EXPERT = D_MODEL
SWIGLU_LIMIT = 7.0
SWIGLU_ALPHA = 1.702
MOE_BLOCK = 128
NORM_EPS = 1e-6

kernel_name = "hybrid_gated_gqa_pool_moe_encoder"


def rms_norm(x, g):
    xf = x.astype(jnp.float32)
    y = xf * lax.rsqrt(jnp.mean(xf * xf, axis=-1, keepdims=True) + NORM_EPS)
    return (y * g.astype(jnp.float32)).astype(x.dtype)


def axial_rope_angles(n_tok):
    n_rows = n_tok // GRID_W
    rows = jnp.repeat(jnp.arange(n_rows, dtype=jnp.float32), GRID_W)
    cols = jnp.tile(jnp.arange(GRID_W, dtype=jnp.float32), n_rows)
    axis_dim = HEAD_DIM // 2
    inv_freq = ROPE_THETA ** (-jnp.arange(0, axis_dim, 2, dtype=jnp.float32) / axis_dim)
    return rows[:, None] * inv_freq, cols[:, None] * inv_freq


def _rotate_half(x, ang):
    half = x.shape[-1] // 2
    c, s = jnp.cos(ang), jnp.sin(ang)
    x1, x2 = x[..., :half], x[..., half:]
    return jnp.concatenate([x1 * c - x2 * s, x2 * c + x1 * s], axis=-1)


def apply_axial_rope(x, ang_row, ang_col):
    a = HEAD_DIM // 2
    xf = x.astype(jnp.float32)
    y = jnp.concatenate([_rotate_half(xf[..., :a], ang_row), _rotate_half(xf[..., a:], ang_col)], axis=-1)
    return y.astype(x.dtype)


def gqa_axial_attention(q, k, v, q_g, k_g, ang_row, ang_col):
    B, S = q.shape[0], q.shape[1]
    q = rms_norm(q, q_g).transpose(0, 2, 1, 3)
    k = rms_norm(k, k_g).transpose(0, 2, 1, 3)
    v = v.transpose(0, 2, 1, 3)
    q = apply_axial_rope(q, ang_row, ang_col)
    k = apply_axial_rope(k, ang_row, ang_col)
    q = q.reshape(B, N_KV_HEADS, Q_PER_KV, S, HEAD_DIM) * (HEAD_DIM ** -0.5)
    n_blk = S // Q_BLOCK
    qb = q.reshape(B, N_KV_HEADS, Q_PER_KV, n_blk, Q_BLOCK, HEAD_DIM).transpose(3, 0, 1, 2, 4, 5)

    def block(qi):
        s = jnp.einsum('bkgqd,bksd->bkgqs', qi, k).astype(jnp.float32)
        p = jax.nn.softmax(s, axis=-1).astype(v.dtype)
        return jnp.einsum('bkgqs,bksd->bkgqd', p, v)

    o = lax.map(block, qb)
    return o.transpose(1, 0, 4, 2, 3, 5).reshape(B, S, ATTN_WIDTH)


def multiscale_pool(u, pool_w, pool_scale):
    B, S, C = u.shape
    uf = u.astype(jnp.float32)
    cs = jnp.concatenate([jnp.zeros((B, 1, C), jnp.float32), jnp.cumsum(uf, axis=1)], axis=1)
    t = jnp.arange(S)
    outs = []
    for gi, w in enumerate(POOL_WINDOWS):
        lo = jnp.clip(t - w // 2, 0, S)
        hi = jnp.clip(t + w - w // 2, 0, S)
        csg = cs[:, :, gi * POOL_GROUP:(gi + 1) * POOL_GROUP]
        mean = (csg[:, hi] - csg[:, lo]) / (hi - lo).astype(jnp.float32)[None, :, None]
        outs.append(mean - uf[:, :, gi * POOL_GROUP:(gi + 1) * POOL_GROUP])
    p = jnp.stack(outs, axis=2)
    y = jnp.einsum('bsgc,gcd->bsgd', p, pool_w.astype(jnp.float32)).reshape(B, S, C)
    return (y * pool_scale.astype(jnp.float32)).astype(u.dtype)


def memory_cross_attention(h, m, w_cq, w_ck, w_cv, w_co):
    B, S, _ = h.shape
    M = m.shape[1]
    q = (h @ w_cq).reshape(B, S, N_MEM_HEADS, MEM_HEAD_DIM)
    k = (m @ w_ck).reshape(B, M, N_MEM_HEADS, MEM_HEAD_DIM)
    v = (m @ w_cv).reshape(B, M, N_MEM_HEADS, MEM_HEAD_DIM)
    s = jnp.einsum('bshd,bmhd->bhsm', q, k).astype(jnp.float32) * (MEM_HEAD_DIM ** -0.5)
    p = jax.nn.softmax(s, axis=-1).astype(v.dtype)
    o = jnp.einsum('bhsm,bmhd->bshd', p, v).reshape(B, S, MEM_WIDTH)
    return o @ w_co


def clamped_swiglu(hgu):
    x_glu = jnp.minimum(hgu[..., ::2], SWIGLU_LIMIT)
    x_lin = jnp.clip(hgu[..., 1::2], -SWIGLU_LIMIT, SWIGLU_LIMIT)
    return x_glu * jax.nn.sigmoid(SWIGLU_ALPHA * x_glu) * (x_lin + 1.0)


def moe(h, w_router, b_router, w_gate_up, b_gate_up, w_down, b_down):
    B, S, D = h.shape
    N = B * S
    A = N * TOP_K
    t = h.reshape(N, D)
    logits = (t @ w_router + b_router).astype(jnp.float32)
    top_val, top_idx = lax.top_k(logits, TOP_K)
    gate = jax.nn.softmax(top_val, axis=-1)
    e_flat = top_idx.reshape(A)
    tok_flat = jnp.repeat(jnp.arange(N, dtype=jnp.int32), TOP_K)
    g_flat = gate.reshape(A)
    order = jnp.argsort(e_flat)
    e_s, tok_s, g_s = e_flat[order], tok_flat[order], g_flat[order]
    counts = jnp.bincount(e_flat, length=N_EXPERTS)
    padded = (counts + MOE_BLOCK - 1) // MOE_BLOCK * MOE_BLOCK
    start = jnp.cumsum(counts) - counts
    pad_end = jnp.cumsum(padded)
    pad_start = pad_end - padded
    slot = pad_start[e_s] + (jnp.arange(A) - start[e_s])
    P = A + N_EXPERTS * MOE_BLOCK
    n_blk = P // MOE_BLOCK
    slot_tok = jnp.full((P,), N, dtype=jnp.int32).at[slot].set(tok_s)
    slot_gate = jnp.zeros((P,), jnp.float32).at[slot].set(g_s)
    blk_expert = jnp.minimum(
        jnp.searchsorted(pad_end, jnp.arange(n_blk) * MOE_BLOCK, side='right'), N_EXPERTS - 1)
    t_pad = jnp.concatenate([t, jnp.zeros((1, D), t.dtype)], axis=0)
    xs = t_pad[slot_tok].reshape(n_blk, MOE_BLOCK, D)

    def expert_block(args):
        xb, e = args
        hgu = xb @ w_gate_up[e] + b_gate_up[e]
        return clamped_swiglu(hgu) @ w_down[e] + b_down[e]

    ys = lax.map(expert_block, (xs, blk_expert)).reshape(P, D)
    ys = ys * slot_gate[:, None].astype(ys.dtype)
    out = jax.ops.segment_sum(ys, slot_tok, num_segments=N + 1)[:N]
    return out.reshape(B, S, D)


def setup_inputs(seed: int = 0) -> dict:
    key = jax.random.key(seed)
    ks = jax.random.split(key, 25)

    def nrm(k, shape, scale):
        return jax.random.normal(k, shape, jnp.float32) * scale

    def gain(k, shape):
        return 1.0 + 0.02 * jax.random.normal(k, shape, jnp.float32)

    L, D = DEPTH, D_MODEL
    return {
        "x": nrm(ks[0], (BATCH, SEQ, D), 1.0),
        "mem": nrm(ks[1], (BATCH, MEM_LEN, D), 1.0),
        "mix_norm_g": gain(ks[2], (L, D)),
        "w_in": nrm(ks[3], (L, D, IN_WIDTH), D ** -0.5),
        "q_norm_g": gain(ks[4], (L, HEAD_DIM)),
        "k_norm_g": gain(ks[5], (L, HEAD_DIM)),
        "pool_w": nrm(ks[6], (L, N_POOL_GROUPS, POOL_GROUP, POOL_GROUP), POOL_GROUP ** -0.5),
        "pool_scale": gain(ks[7], (L, POOL_WIDTH)),
        "w_attn_up": nrm(ks[8], (L, ATTN_WIDTH, D), ATTN_WIDTH ** -0.5),
        "w_pool_up": nrm(ks[9], (L, POOL_WIDTH, D), POOL_WIDTH ** -0.5),
        "w_mix_out": nrm(ks[10], (L, D, D), D ** -0.5),
        "cross_norm_g": gain(ks[11], (L, D)),
        "mem_norm_g": gain(ks[12], (L, D)),
        "w_cq": nrm(ks[13], (L, D, MEM_WIDTH), D ** -0.5),
        "w_ck": nrm(ks[14], (L, D, MEM_WIDTH), D ** -0.5),
        "w_cv": nrm(ks[15], (L, D, MEM_WIDTH), D ** -0.5),
        "w_co": nrm(ks[16], (L, MEM_WIDTH, D), MEM_WIDTH ** -0.5),
        "moe_norm_g": gain(ks[17], (L, D)),
        "w_router": nrm(ks[18], (L, D, N_EXPERTS), D ** -0.5),
        "b_router": nrm(ks[19], (L, N_EXPERTS), 0.01),
        "w_gate_up": nrm(ks[20], (L, N_EXPERTS, D, 2 * D_EXPERT), D ** -0.5),
        "b_gate_up": nrm(ks[21], (L, N_EXPERTS, 2 * D_EXPERT), 0.01),
        "w_down": nrm(ks[22], (L, N_EXPERTS, D_EXPERT, D), D_EXPERT ** -0.5),
        "b_down": nrm(ks[23], (L, N_EXPERTS, D), 0.01),
        "final_norm_g": gain(ks[24], (D,)),
    }


def reference(x, mem, mix_norm_g, w_in, q_norm_g, k_norm_g, pool_w, pool_scale, w_attn_up,
              w_pool_up, w_mix_out, cross_norm_g, mem_norm_g, w_cq, w_ck, w_cv, w_co,
              moe_norm_g, w_router, b_router, w_gate_up, b_gate_up, w_down, b_down, final_norm_g):
    B, S, _ = x.shape
    ang_row, ang_col = axial_rope_angles(S)
    for l in range(DEPTH):
        h = rms_norm(x, mix_norm_g[l])
        proj = h @ w_in[l]
        q = proj[..., :Q_END].reshape(B, S, N_Q_HEADS, HEAD_DIM)
        k = proj[..., Q_END:K_END].reshape(B, S, N_KV_HEADS, HEAD_DIM)
        v = proj[..., K_END:V_END].reshape(B, S, N_KV_HEADS, HEAD_DIM)
        u = proj[..., V_END:P_END]
        gate_attn = jax.nn.sigmoid(proj[..., P_END:GA_END])
        gate_pool = jax.nn.sigmoid(proj[..., GA_END:])
        attn = gqa_axial_attention(q, k, v, q_norm_g[l], k_norm_g[l], ang_row, ang_col)
        pool = multiscale_pool(u, pool_w[l], pool_scale[l])
        mixed = gate_attn * (attn @ w_attn_up[l]) + gate_pool * (pool @ w_pool_up[l])
        x = x + mixed @ w_mix_out[l]
        h = rms_norm(x, cross_norm_g[l])
        m = rms_norm(mem, mem_norm_g[l])
        x = x + memory_cross_attention(h, m, w_cq[l], w_ck[l], w_cv[l], w_co[l])
        h = rms_norm(x, moe_norm_g[l])
        x = x + moe(h, w_router[l], b_router[l], w_gate_up[l], b_gate_up[l], w_down[l], b_down[l])
    return rms_norm(x, final_norm_g)
```

```python
import functools

import jax
import jax.numpy as jnp
from jax import lax
from jax.experimental import pallas as pl
from jax.experimental.pallas import tpu as pltpu

F32 = jnp.float32
BF16 = jnp.bfloat16

D_MODEL = 2048
GRID_W = 64
HEAD_DIM = 128
N_Q_HEADS = 8
N_KV_HEADS = 2
Q_PER_KV = N_Q_HEADS // N_KV_HEADS
ATTN_WIDTH = N_Q_HEADS * HEAD_DIM
KV_WIDTH = N_KV_HEADS * HEAD_DIM
ROPE_THETA = 10000.0
POOL_WINDOWS = (2, 4, 8, 16)
N_POOL_GROUPS = len(POOL_WINDOWS)
POOL_WIDTH = D_MODEL // 2
POOL_GROUP = POOL_WIDTH // N_POOL_GROUPS
Q_END = ATTN_WIDTH
K_END = Q_END + KV_WIDTH
V_END = K_END + KV_WIDTH
P_END = V_END + POOL_WIDTH
GA_END = P_END + D_MODEL
IN_WIDTH = GA_END + D_MODEL
N_MEM_HEADS = 4
MEM_HEAD_DIM = 128
MEM_WIDTH = N_MEM_HEADS * MEM_HEAD_DIM
N_EXPERTS = 32
TOP_K = 4
D_EXPERT = D_MODEL
SWIGLU_LIMIT = 7.0
SWIGLU_ALPHA = 1.702
NORM_EPS = 1e-6

LANES = 128
BF16_ROWS = 16
VMEM_LIMIT = 56 * 1024 * 1024

PJ_GA = 0
PJ_GP = D_MODEL
PJ_U = 2 * D_MODEL
PJ_Q = PJ_U + POOL_WIDTH
PJ_K = PJ_Q + ATTN_WIDTH
PJ_V = PJ_K + KV_WIDTH

PROJ_TM = 1024
PROJ_TN = 512
ATTN_TQ = 512
MIX_TM = 256
HALO = BF16_ROWS
CROSS_TM = 256
ROUTE_CHUNK = 256
SLOT_BLK = 256
SUB_PER_ITEM = 4
FFN_CHUNK = 256
DISPATCH_CHUNK = 256
COMBINE_TM = 256


def _cparams(sem, vmem=VMEM_LIMIT):
    return pltpu.CompilerParams(dimension_semantics=sem, vmem_limit_bytes=vmem)


def _rms(x, g):
    return x * lax.rsqrt(jnp.mean(x * x, axis=-1, keepdims=True) + NORM_EPS) * g


def _proj_kernel(x_ref, g_ref, w_ref, o_ref, h_ref, *, n_gate_blocks):
    j = pl.program_id(1)

    @pl.when(j == 0)
    def _():
        h_ref[...] = _rms(x_ref[...], g_ref[...]).astype(BF16)

    acc = jnp.dot(h_ref[...], w_ref[...], preferred_element_type=F32)

    @pl.when(j < n_gate_blocks)
    def _():
        o_ref[...] = (1.0 / (1.0 + jnp.exp(-acc))).astype(BF16)

    @pl.when(j >= n_gate_blocks)
    def _():
        o_ref[...] = acc.astype(BF16)


def _proj(x2, g, w):
    n, d = x2.shape
    width = w.shape[1]
    return pl.pallas_call(
        functools.partial(_proj_kernel, n_gate_blocks=PJ_U // PROJ_TN),
        grid=(n // PROJ_TM, width // PROJ_TN),
        in_specs=[
            pl.BlockSpec((PROJ_TM, d), lambda i, j: (i, 0)),
            pl.BlockSpec((1, d), lambda i, j: (0, 0)),
            pl.BlockSpec((d, PROJ_TN), lambda i, j: (0, j)),
        ],
        out_specs=pl.BlockSpec((PROJ_TM, PROJ_TN), lambda i, j: (i, j)),
        out_shape=jax.ShapeDtypeStruct((n, width), BF16),
        scratch_shapes=[pltpu.VMEM((PROJ_TM, d), BF16)],
        compiler_params=_cparams(("parallel", "arbitrary")),
        name="proj",
    )(x2, g, w)


def _rope(x, cos, sin_signed):
    lane = lax.broadcasted_iota(jnp.int32, x.shape, 1)
    low = (lane % (HEAD_DIM // 2)) < (HEAD_DIM // 4)
    partner = jnp.where(low, pltpu.roll(x, HEAD_DIM - HEAD_DIM // 4, 1), pltpu.roll(x, HEAD_DIM // 4, 1))
    return x * cos + partner * sin_signed


def _attn_kernel(q_ref, k_ref, v_ref, cos_ref, sin_ref, qg_ref, kg_ref, o_ref, kp_ref, *, tq):
    h = pl.program_id(1)
    qi = pl.program_id(2)

    @pl.when((qi == 0) & (h % Q_PER_KV == 0))
    def _():
        kn = _rms(k_ref[...].astype(F32), kg_ref[...])
        kp_ref[...] = _rope(kn, cos_ref[...], sin_ref[...]).astype(BF16)

    rows = pl.ds(pl.multiple_of(qi * tq, tq), tq)
    qn = _rms(q_ref[...].astype(F32), qg_ref[...])
    qr = (_rope(qn, cos_ref[rows, :], sin_ref[rows, :]) * (HEAD_DIM ** -0.5)).astype(BF16)
    s = lax.dot_general(qr, kp_ref[...], (((1,), (1,)), ((), ())), preferred_element_type=F32)
    m = jnp.max(s, axis=-1, keepdims=True)
    p = jnp.exp(s - m)
    l = jnp.sum(p, axis=-1, keepdims=True)
    o = jnp.dot(p.astype(BF16), v_ref[...], preferred_element_type=F32)
    o_ref[...] = (o / l).astype(BF16)


def _attention(proj, cos, sin_signed, q_g, k_g, batch, seq):
    tq = ATTN_TQ
    nq = seq // tq
    qc, kc, vc = PJ_Q // HEAD_DIM, PJ_K // HEAD_DIM, PJ_V // HEAD_DIM
    return pl.pallas_call(
        functools.partial(_attn_kernel, tq=tq),
        grid=(batch, N_Q_HEADS, nq),
        in_specs=[
            pl.BlockSpec((tq, HEAD_DIM), lambda b, h, i: (b * nq + i, qc + h)),
            pl.BlockSpec((seq, HEAD_DIM), lambda b, h, i: (b, kc + h // Q_PER_KV)),
            pl.BlockSpec((seq, HEAD_DIM), lambda b, h, i: (b, vc + h // Q_PER_KV)),
            pl.BlockSpec((seq, HEAD_DIM), lambda b, h, i: (0, 0)),
            pl.BlockSpec((seq, HEAD_DIM), lambda b, h, i: (0, 0)),
            pl.BlockSpec((1, HEAD_DIM), lambda b, h, i: (0, 0)),
            pl.BlockSpec((1, HEAD_DIM), lambda b, h, i: (0, 0)),
        ],
        out_specs=pl.BlockSpec((tq, HEAD_DIM), lambda b, h, i: (b * nq + i, h)),
        out_shape=jax.ShapeDtypeStruct((batch * seq, ATTN_WIDTH), BF16),
        scratch_shapes=[pltpu.VMEM((seq, HEAD_DIM), BF16)],
        compiler_params=_cparams(("arbitrary", "arbitrary", "arbitrary")),
        name="attn",
    )(proj, proj, proj, cos, sin_signed, q_g, k_g)


def _mix_kernel(attn_ref, u_ref, up_ref, un_ref, ga_ref, gp_ref, x_ref, pw_ref, ps_ref, wa_ref, wp_ref,
                wo_ref, o_ref, *, tm, seq):
    i = pl.program_id(0)
    t0 = (i * tm) % seq
    u = u_ref[...]
    prev = up_ref[...]
    nxt = un_ref[...]
    has_prev = t0 > 0
    has_next = t0 + tm < seq
    r = lax.broadcasted_iota(jnp.int32, (tm, tm), 0)
    c = lax.broadcasted_iota(jnp.int32, (tm, tm), 1)
    rh = lax.broadcasted_iota(jnp.int32, (tm, HALO), 0)
    ch = lax.broadcasted_iota(jnp.int32, (tm, HALO), 1)
    tpos = t0 + lax.broadcasted_iota(jnp.int32, (tm, 1), 0)
    ys = []
    for gi, w in enumerate(POOL_WINDOWS):
        sl = slice(gi * POOL_GROUP, (gi + 1) * POOL_GROUP)
        lo, hi = w // 2, w - w // 2
        band = jnp.where((c >= r - lo) & (c < r + hi), 1.0, 0.0).astype(BF16)
        band_p = jnp.where(has_prev & (ch - HALO >= rh - lo), 1.0, 0.0).astype(BF16)
        band_n = jnp.where(has_next & (ch + tm < rh + hi), 1.0, 0.0).astype(BF16)
        sums = (jnp.dot(band, u[:, sl], preferred_element_type=F32)
                + jnp.dot(band_p, prev[:, sl], preferred_element_type=F32)
                + jnp.dot(band_n, nxt[:, sl], preferred_element_type=F32))
        cnt = jnp.minimum(tpos + (w - w // 2), seq) - jnp.maximum(tpos - w // 2, 0)
        pooled = sums / cnt.astype(F32) - u[:, sl].astype(F32)
        ys.append(jnp.dot(pooled.astype(BF16), pw_ref[gi], preferred_element_type=F32))
    pool = (jnp.concatenate(ys, axis=1) * ps_ref[...]).astype(BF16)
    lift_a = jnp.dot(attn_ref[...], wa_ref[...], preferred_element_type=F32)
    lift_p = jnp.dot(pool, wp_ref[...], preferred_element_type=F32)
    mixed = ga_ref[...].astype(F32) * lift_a + gp_ref[...].astype(F32) * lift_p
    o_ref[...] = x_ref[...] + jnp.dot(mixed.astype(BF16), wo_ref[...], preferred_element_type=F32)


def _resident(shape):
    nd = len(shape)
    return pl.BlockSpec(shape, lambda *_: (0,) * nd, pipeline_mode=pl.Buffered(1))


def _mix(attn, proj, x2, pool_w, pool_scale, w_attn_up, w_pool_up, w_mix_out, seq):
    n = x2.shape[0]
    tm = MIX_TM
    hb = tm // HALO
    last_halo = n // HALO - 1
    return pl.pallas_call(
        functools.partial(_mix_kernel, tm=tm, seq=seq),
        grid=(n // tm,),
        in_specs=[
            pl.BlockSpec((tm, ATTN_WIDTH), lambda i: (i, 0)),
            pl.BlockSpec((tm, POOL_WIDTH), lambda i: (i, PJ_U // POOL_WIDTH)),
            pl.BlockSpec((HALO, POOL_WIDTH), lambda i: (jnp.maximum(i * hb - 1, 0), PJ_U // POOL_WIDTH)),
            pl.BlockSpec((HALO, POOL_WIDTH), lambda i: (jnp.minimum((i + 1) * hb, last_halo), PJ_U // POOL_WIDTH)),
            pl.BlockSpec((tm, D_MODEL), lambda i: (i, PJ_GA // D_MODEL)),
            pl.BlockSpec((tm, D_MODEL), lambda i: (i, PJ_GP // D_MODEL)),
            pl.BlockSpec((tm, D_MODEL), lambda i: (i, 0)),
            _resident(pool_w.shape),
            _resident(pool_scale.shape),
            _resident(w_attn_up.shape),
            _resident(w_pool_up.shape),
            _resident(w_mix_out.shape),
        ],
        out_specs=pl.BlockSpec((tm, D_MODEL), lambda i: (i, 0)),
        out_shape=jax.ShapeDtypeStruct((n, D_MODEL), F32),
        compiler_params=_cparams(("parallel",)),
        name="mix",
    )(attn, proj, proj, proj, proj, proj, x2, pool_w, pool_scale, w_attn_up, w_pool_up, w_mix_out)


def _memkv_kernel(m_ref, g_ref, wk_ref, wv_ref, k_ref, v_ref):
    m = _rms(m_ref[0], g_ref[...]).astype(BF16)
    k_ref[0] = jnp.dot(m, wk_ref[...], preferred_element_type=F32).astype(BF16)
    v_ref[0] = jnp.dot(m, wv_ref[...], preferred_element_type=F32).astype(BF16)


def _memkv(mem, g, w_ck, w_cv):
    b, m, d = mem.shape
    out = jax.ShapeDtypeStruct((b, m, MEM_WIDTH), BF16)
    return pl.pallas_call(
        _memkv_kernel,
        grid=(b,),
        in_specs=[
            pl.BlockSpec((1, m, d), lambda i: (i, 0, 0)),
            pl.BlockSpec((1, d), lambda i: (0, 0)),
            pl.BlockSpec((d, MEM_WIDTH), lambda i: (0, 0)),
            pl.BlockSpec((d, MEM_WIDTH), lambda i: (0, 0)),
        ],
        out_specs=[pl.BlockSpec((1, m, MEM_WIDTH), lambda i: (i, 0, 0))] * 2,
        out_shape=[out, out],
        compiler_params=_cparams(("parallel",)),
        name="memkv",
    )(mem, g, w_ck, w_cv)


def _split_bf16(a):
    hi = a.astype(BF16)
    lo = (a - hi.astype(F32)).astype(BF16)
    return hi, lo


def _cross_kernel(x_ref, cg_ref, wq_ref, k_ref, v_ref, wo_ref, mg_ref, wr_ref, br_ref,
                  x2_ref, hp_ref, lt_ref):
    x = x_ref[...]
    h = _rms(x, cg_ref[...]).astype(BF16)
    q = jnp.dot(h, wq_ref[...], preferred_element_type=F32)
    k = k_ref[0]
    v = v_ref[0]
    outs = []
    for hd in range(N_MEM_HEADS):
        sl = slice(hd * MEM_HEAD_DIM, (hd + 1) * MEM_HEAD_DIM)
        s = lax.dot_general(q[:, sl].astype(BF16), k[:, sl], (((1,), (1,)), ((), ())),
                            preferred_element_type=F32) * (MEM_HEAD_DIM ** -0.5)
        p = jnp.exp(s - jnp.max(s, axis=-1, keepdims=True))
        l = jnp.sum(p, axis=-1, keepdims=True)
        outs.append(jnp.dot(p.astype(BF16), v[:, sl], preferred_element_type=F32) / l)
    o = jnp.concatenate(outs, axis=1).astype(BF16)
    x2 = x + jnp.dot(o, wo_ref[...], preferred_element_type=F32)
    x2_ref[...] = x2

    h3 = _rms(x2, mg_ref[...])
    h_hi, h_lo = _split_bf16(h3)
    half = D_MODEL // 2
    bits = pltpu.bitcast(h_hi.astype(F32), jnp.uint32)
    hp_ref[...] = (bits[:, :half] >> 16) | (bits[:, half:] & jnp.uint32(0xFFFF0000))

    w_hi, w_lo = _split_bf16(wr_ref[...])
    logits = (jnp.dot(h_hi, w_hi, preferred_element_type=F32)
              + jnp.dot(h_hi, w_lo, preferred_element_type=F32)
              + jnp.dot(h_lo, w_hi, preferred_element_type=F32)) + br_ref[...]
    lt_ref[...] = logits.T[:N_EXPERTS, :]


def _cross(x1, cross_g, w_cq, kmem, vmem, w_co, moe_g, w_router_pad, b_router_pad, seq):
    n = x1.shape[0]
    tm = CROSS_TM
    per_seq = seq // tm
    mlen = kmem.shape[1]
    return pl.pallas_call(
        _cross_kernel,
        grid=(n // tm,),
        in_specs=[
            pl.BlockSpec((tm, D_MODEL), lambda i: (i, 0)),
            _resident((1, D_MODEL)),
            _resident(w_cq.shape),
            pl.BlockSpec((1, mlen, MEM_WIDTH), lambda i: (i // per_seq, 0, 0)),
            pl.BlockSpec((1, mlen, MEM_WIDTH), lambda i: (i // per_seq, 0, 0)),
            _resident(w_co.shape),
            _resident((1, D_MODEL)),
            _resident(w_router_pad.shape),
            _resident(b_router_pad.shape),
        ],
        out_specs=[
            pl.BlockSpec((tm, D_MODEL), lambda i: (i, 0)),
            pl.BlockSpec((tm, D_MODEL // 2), lambda i: (i, 0)),
            pl.BlockSpec((N_EXPERTS, tm), lambda i: (0, i)),
        ],
        out_shape=[
            jax.ShapeDtypeStruct((n, D_MODEL), F32),
            jax.ShapeDtypeStruct((n, D_MODEL // 2), jnp.uint32),
            jax.ShapeDtypeStruct((N_EXPERTS, n), F32),
        ],
        compiler_params=_cparams(("parallel",)),
        name="cross",
    )(x1, cross_g, w_cq, kmem, vmem, w_co, moe_g, w_router_pad, b_router_pad)


def _route_kernel(lt_ref, pos_ref, gate_ref, cnt_ref, nblk_ref, start_ref, m_ref, rank_ref):
    n = lt_ref.shape[1]
    l = lt_ref[...]
    row = lax.broadcasted_iota(jnp.int32, l.shape, 0)
    sels, vals = [], []
    for _ in range(TOP_K):
        m = jnp.max(l, axis=0, keepdims=True)
        idx = jnp.min(jnp.where(l == m, row, N_EXPERTS), axis=0, keepdims=True)
        sel = row == idx
        sels.append(sel)
        vals.append(m)
        l = jnp.where(sel, -jnp.inf, l)

    es = [jnp.exp(v - vals[0]) for v in vals]
    den = es[0] + es[1] + es[2] + es[3]
    for k in range(TOP_K):
        gate_ref[k:k + 1, :] = es[k] / den

    chosen = sels[0] | sels[1] | sels[2] | sels[3]
    m_ref[...] = jnp.where(chosen, 1.0, 0.0)

    ci = lax.broadcasted_iota(jnp.int32, (ROUTE_CHUNK, ROUTE_CHUNK), 0)
    cj = lax.broadcasted_iota(jnp.int32, (ROUTE_CHUNK, ROUTE_CHUNK), 1)
    upper = jnp.where(ci < cj, 1.0, 0.0).astype(BF16)

    def chunk(c, carry):
        cols = pl.ds(pl.multiple_of(c * ROUTE_CHUNK, ROUTE_CHUNK), ROUTE_CHUNK)
        mt = m_ref[:, cols]
        rank_ref[:, cols] = jnp.dot(mt.astype(BF16), upper, preferred_element_type=F32) + carry
        return carry + jnp.sum(mt, axis=1, keepdims=True)

    counts = lax.fori_loop(0, n // ROUTE_CHUNK, chunk, jnp.zeros((N_EXPERTS, 1), F32))
    nblk = jnp.floor((counts + (SLOT_BLK - 1)) * (1.0 / SLOT_BLK))
    ei = lax.broadcasted_iota(jnp.int32, (N_EXPERTS, N_EXPERTS), 0)
    ej = lax.broadcasted_iota(jnp.int32, (N_EXPERTS, N_EXPERTS), 1)
    lower = jnp.where(ej < ei, 1.0, 0.0).astype(BF16)
    nblk_b = jnp.broadcast_to(nblk, (N_EXPERTS, LANES))
    start_b = jnp.dot(lower, nblk_b.astype(BF16), preferred_element_type=F32)
    cnt_ref[...] = jnp.broadcast_to(counts, (N_EXPERTS, LANES)).astype(jnp.int32)
    nblk_ref[...] = nblk_b.astype(jnp.int32)
    start_ref[...] = start_b.astype(jnp.int32)

    slot = start_b[:, :1] * SLOT_BLK + rank_ref[...]
    for k in range(TOP_K):
        pos_ref[k:k + 1, :] = jnp.sum(jnp.where(sels[k], slot, 0.0), axis=0, keepdims=True).astype(jnp.int32)


def _route(logits_t):
    e, n = logits_t.shape
    small = jax.ShapeDtypeStruct((e, LANES), jnp.int32)
    return pl.pallas_call(
        _route_kernel,
        out_shape=[
            jax.ShapeDtypeStruct((TOP_K, n), jnp.int32),
            jax.ShapeDtypeStruct((TOP_K, n), F32),
            small, small, small,
        ],
        scratch_shapes=[pltpu.VMEM((e, n), F32), pltpu.VMEM((e, n), F32)],
        compiler_params=pltpu.CompilerParams(vmem_limit_bytes=VMEM_LIMIT),
        name="route",
    )(logits_t)


def _dispatch_kernel(pos_ref, cnt_ref, nblk_ref, start_ref, h_ref, xs_ref, zero_ref, sem, pad_sem, *, n_tok):
    zero_ref[...] = jnp.zeros(zero_ref.shape, zero_ref.dtype)
    rows_per_round = DISPATCH_CHUNK * TOP_K

    def wait_round():
        pltpu.make_async_copy(h_ref.at[pl.ds(0, rows_per_round)], xs_ref.at[pl.ds(0, rows_per_round)], sem).wait()

    def issue_round(c, _):
        def tok(t, _):
            for k in range(TOP_K):
                p = pos_ref[t * TOP_K + k]
                pltpu.make_async_copy(h_ref.at[pl.ds(t, 1)], xs_ref.at[pl.ds(p, 1)], sem).start()
            return 0

        lax.fori_loop(c * DISPATCH_CHUNK, (c + 1) * DISPATCH_CHUNK, tok, 0)

        @pl.when(c > 0)
        def _():
            wait_round()
        return 0

    lax.fori_loop(0, n_tok // DISPATCH_CHUNK, issue_round, 0)
    wait_round()

    def pad_expert(e, _):
        lo = start_ref[e] * SLOT_BLK + cnt_ref[e]
        hi = (start_ref[e] + nblk_ref[e]) * SLOT_BLK

        def fill(r, _):
            pltpu.make_async_copy(zero_ref.at[pl.ds(0, 1)], xs_ref.at[pl.ds(r, 1)], pad_sem).start()
            return 0

        def drain(r, _):
            pltpu.make_async_copy(zero_ref.at[pl.ds(0, 1)], xs_ref.at[pl.ds(r, 1)], pad_sem).wait()
            return 0

        lax.fori_loop(lo, hi, fill, 0)
        lax.fori_loop(lo, hi, drain, 0)
        return 0

    lax.fori_loop(0, N_EXPERTS, pad_expert, 0)

    def tail(b, _):
        rows = pl.ds(pl.multiple_of(b * SLOT_BLK, SLOT_BLK), SLOT_BLK)
        cp = pltpu.make_async_copy(zero_ref, xs_ref.at[rows], pad_sem)
        cp.start()
        cp.wait()
        return 0

    lax.fori_loop(start_ref[N_EXPERTS - 1] + nblk_ref[N_EXPERTS - 1], xs_ref.shape[0] // SLOT_BLK, tail, 0)


def _dispatch(pos_flat, counts, nblk, start, h_packed, n_slots):
    n_tok, width = h_packed.shape
    return pl.pallas_call(
        functools.partial(_dispatch_kernel, n_tok=n_tok),
        grid_spec=pltpu.PrefetchScalarGridSpec(
            num_scalar_prefetch=4,
            grid=(1,),
            in_specs=[pl.BlockSpec(memory_space=pl.ANY)],
            out_specs=pl.BlockSpec(memory_space=pl.ANY),
            scratch_shapes=[
                pltpu.VMEM((SLOT_BLK, width), h_packed.dtype),
                pltpu.SemaphoreType.DMA(()),
                pltpu.SemaphoreType.DMA(()),
            ],
        ),
        out_shape=jax.ShapeDtypeStruct((n_slots, width), h_packed.dtype),
        compiler_params=pltpu.CompilerParams(dimension_semantics=("arbitrary",), has_side_effects=True),
        name="dispatch",
    )(pos_flat, counts, nblk, start, h_packed)


def _swiglu_pairs(h_a, h_b):
    lane = lax.broadcasted_iota(jnp.int32, h_a.shape, 1)
    even = (lane % 2) == 0
    gate_in = jnp.where(even, h_a, pltpu.roll(h_b, 1, 1))
    lin_in = jnp.where(even, pltpu.roll(h_a, LANES - 1, 1), h_b)
    glu = jnp.minimum(gate_in, SWIGLU_LIMIT)
    lin = jnp.clip(lin_in, -SWIGLU_LIMIT, SWIGLU_LIMIT)
    return glu * (1.0 / (1.0 + jnp.exp(-SWIGLU_ALPHA * glu))) * (lin + 1.0)


def _experts_kernel(ie_ref, ib_ref, ins_ref, ni_ref, *refs):
    x_refs = refs[:SUB_PER_ITEM]
    w1_ref, b1_ref, w2_ref, b2_ref, ys_ref, xb_ref, w2q_ref, acc_ref, sem = refs[SUB_PER_ITEM:]
    w = pl.program_id(0)
    j = pl.program_id(1)
    last_j = pl.num_programs(1) - 1
    n_items = ni_ref[0]
    nsub = ins_ref[w]
    slot = w % 2
    half = D_MODEL // 2
    quarter = FFN_CHUNK // 4

    def writeback(item, item_slot, s):
        rows = pl.ds(pl.multiple_of((ib_ref[item] + s) * SLOT_BLK, SLOT_BLK), SLOT_BLK)
        return pltpu.make_async_copy(acc_ref.at[item_slot, s], ys_ref.at[rows], sem.at[item_slot])

    def wait_item(item, item_slot):
        for s in range(SUB_PER_ITEM):
            @pl.when(s < ins_ref[item])
            def _(s=s):
                writeback(item, item_slot, s).wait()

    @pl.when(w < n_items)
    def _():
        @pl.when((j == 0) & (w >= 2))
        def _():
            wait_item(w - 2, slot)

        w1 = w1_ref[...].astype(BF16)
        for dst, src in ((0, 0), (1, 2 * quarter), (2 * quarter, quarter), (2 * quarter + 1, 3 * quarter)):
            for c in range(D_MODEL // LANES):
                w2q_ref[c, pl.ds(dst, quarter, stride=2), :] = w2_ref[pl.ds(src, quarter), c * LANES:(c + 1) * LANES]
        w2 = jnp.concatenate([w2q_ref[c] for c in range(D_MODEL // LANES)], axis=1).astype(BF16)
        b1 = b1_ref[...]
        b2 = b2_ref[...]

        for s in range(SUB_PER_ITEM):
            @pl.when(s < nsub)
            def _(s=s):
                @pl.when(j == 0)
                def _():
                    word = x_refs[s][...]
                    xb_ref[s, :, :half] = pltpu.bitcast(word << 16, F32).astype(BF16)
                    xb_ref[s, :, half:] = pltpu.bitcast(word & jnp.uint32(0xFFFF0000), F32).astype(BF16)

                h = jnp.dot(xb_ref[s], w1, preferred_element_type=F32) + b1
                act = jnp.concatenate(
                    [_swiglu_pairs(h[:, 0:LANES], h[:, 2 * LANES:3 * LANES]),
                     _swiglu_pairs(h[:, LANES:2 * LANES], h[:, 3 * LANES:4 * LANES])], axis=1)
                contrib = jnp.dot(act.astype(BF16), w2, preferred_element_type=F32)

                @pl.when(j == 0)
                def _():
                    acc_ref[slot, s] = contrib + b2

                @pl.when(j > 0)
                def _():
                    acc_ref[slot, s] += contrib

                @pl.when(j == last_j)
                def _():
                    writeback(w, slot, s).start()

        @pl.when((j == last_j) & (w == n_items - 1))
        def _():
            wait_item(w, slot)

            @pl.when(w >= 1)
            def _():
                wait_item(w - 1, 1 - slot)

            zero_blk = acc_ref.at[slot, 0]
            zero_blk[...] = jnp.zeros(zero_blk.shape, F32)

            def tail(b, _):
                rows = pl.ds(pl.multiple_of(b * SLOT_BLK, SLOT_BLK), SLOT_BLK)
                cp = pltpu.make_async_copy(zero_blk, ys_ref.at[rows], sem.at[slot])
                cp.start()
                cp.wait()
                return 0

            lax.fori_loop(ib_ref[w] + nsub, ys_ref.shape[0] // SLOT_BLK, tail, 0)


def _experts(item_e, item_b, item_n, n_items, xs, w_gate_up, b_gate_up, w_down, b_down, n_blocks, max_items):
    width = xs.shape[1]
    n_chunks = D_EXPERT // FFN_CHUNK

    def x_spec(s):
        return pl.BlockSpec((SLOT_BLK, width),
                            lambda w, j, ie, ib, ins, ni: (ib[w] + jnp.minimum(s, ins[w] - 1), 0))

    in_specs = [x_spec(s) for s in range(SUB_PER_ITEM)] + [
        pl.BlockSpec((None, D_MODEL, 2 * FFN_CHUNK), lambda w, j, ie, ib, ins, ni: (ie[w], 0, j)),
        pl.BlockSpec((None, 1, 2 * FFN_CHUNK), lambda w, j, ie, ib, ins, ni: (ie[w], 0, j)),
        pl.BlockSpec((None, FFN_CHUNK, D_MODEL), lambda w, j, ie, ib, ins, ni: (ie[w], j, 0)),
        pl.BlockSpec((None, 1, D_MODEL), lambda w, j, ie, ib, ins, ni: (ie[w], 0, 0)),
    ]
    return pl.pallas_call(
        _experts_kernel,
        grid_spec=pltpu.PrefetchScalarGridSpec(
            num_scalar_prefetch=4,
            grid=(max_items, n_chunks),
            in_specs=in_specs,
            out_specs=pl.BlockSpec(memory_space=pl.ANY),
            scratch_shapes=[
                pltpu.VMEM((SUB_PER_ITEM, SLOT_BLK, D_MODEL), BF16),
                pltpu.VMEM((D_MODEL // LANES, FFN_CHUNK, LANES), F32),
                pltpu.VMEM((2, SUB_PER_ITEM, SLOT_BLK, D_MODEL), F32),
                pltpu.SemaphoreType.DMA((2,)),
            ],
        ),
        out_shape=jax.ShapeDtypeStruct((n_blocks * SLOT_BLK, D_MODEL), F32),
        compiler_params=_cparams(("arbitrary", "arbitrary")),
        name="experts",
    )(item_e, item_b, item_n, n_items, *([xs] * SUB_PER_ITEM), w_gate_up, b_gate_up, w_down, b_down)


def _combine_kernel(pos_ref, ys_ref, gate_ref, x_ref, g_ref, o_ref, buf_ref, sem, *, tm):
    i = pl.program_id(0)
    n_steps = pl.num_programs(0)

    def issue(step, slot):
        def tok(r, _):
            t = step * tm + r
            for k in range(TOP_K):
                p = pos_ref[t * TOP_K + k]
                pltpu.make_async_copy(ys_ref.at[pl.ds(p, 1)], buf_ref.at[slot, k, pl.ds(r, 1)], sem.at[slot]).start()
            return 0
        lax.fori_loop(0, tm, tok, 0)

    @pl.when(i == 0)
    def _():
        issue(0, 0)

    @pl.when(i + 1 < n_steps)
    def _():
        issue(i + 1, (i + 1) % 2)

    slot = i % 2
    for k in range(TOP_K):
        pltpu.make_async_copy(ys_ref.at[pl.ds(0, tm)], buf_ref.at[slot, k], sem.at[slot]).wait()
    gates = gate_ref[...]
    acc = x_ref[...]
    for k in range(TOP_K):
        acc = acc + gates[:, k:k + 1] * buf_ref[slot, k]
    o_ref[...] = _rms(acc, g_ref[...])


def _combine(pos_flat, ys, gates, x2, final_g):
    n = x2.shape[0]
    tm = COMBINE_TM
    return pl.pallas_call(
        functools.partial(_combine_kernel, tm=tm),
        grid_spec=pltpu.PrefetchScalarGridSpec(
            num_scalar_prefetch=1,
            grid=(n // tm,),
            in_specs=[
                pl.BlockSpec(memory_space=pl.ANY),
                pl.BlockSpec((tm, TOP_K), lambda i, pos: (i, 0)),
                pl.BlockSpec((tm, D_MODEL), lambda i, pos: (i, 0)),
                pl.BlockSpec((1, D_MODEL), lambda i, pos: (0, 0)),
            ],
            out_specs=pl.BlockSpec((tm, D_MODEL), lambda i, pos: (i, 0)),
            scratch_shapes=[
                pltpu.VMEM((2, TOP_K, tm, D_MODEL), F32),
                pltpu.SemaphoreType.DMA((2,)),
            ],
        ),
        out_shape=jax.ShapeDtypeStruct((n, D_MODEL), F32),
        compiler_params=_cparams(("arbitrary",)),
        name="combine",
    )(pos_flat, ys, gates, x2, final_g)


def _rope_tables(seq):
    t = jnp.arange(seq, dtype=jnp.int32)
    rows = (t // GRID_W).astype(F32)
    cols = (t % GRID_W).astype(F32)
    axis_dim = HEAD_DIM // 2
    inv_freq = ROPE_THETA ** (-jnp.arange(0, axis_dim, 2, dtype=F32) / axis_dim)
    ar = rows[:, None] * inv_freq
    ac = cols[:, None] * inv_freq
    cos = jnp.concatenate([jnp.cos(ar), jnp.cos(ar), jnp.cos(ac), jnp.cos(ac)], axis=1)
    sin = jnp.concatenate([-jnp.sin(ar), jnp.sin(ar), -jnp.sin(ac), jnp.sin(ac)], axis=1)
    return cos, sin


def _work_items(nblk, start, max_items):
    per_e = (nblk + SUB_PER_ITEM - 1) // SUB_PER_ITEM
    ends = jnp.cumsum(per_e)
    n_items = ends[-1]
    w = jnp.minimum(jnp.arange(max_items, dtype=jnp.int32), n_items - 1)
    e = jnp.minimum(jnp.searchsorted(ends, w, side="right"), N_EXPERTS - 1).astype(jnp.int32)
    local = w - (ends - per_e)[e]
    blk0 = start[e] + local * SUB_PER_ITEM
    nsub = jnp.clip(nblk[e] - local * SUB_PER_ITEM, 1, SUB_PER_ITEM)
    return e, blk0.astype(jnp.int32), nsub.astype(jnp.int32), n_items.reshape(1).astype(jnp.int32)


def _layer(x2, mem, p, batch, seq, cos, sin_signed):
    n = x2.shape[0]
    w_in = p["w_in"]
    w_in_r = jnp.concatenate(
        [w_in[:, P_END:GA_END], w_in[:, GA_END:], w_in[:, V_END:P_END], w_in[:, :V_END]], axis=1).astype(BF16)
    proj = _proj(x2, p["mix_norm_g"][None, :], w_in_r)
    attn = _attention(proj, cos, sin_signed, p["q_norm_g"][None, :], p["k_norm_g"][None, :], batch, seq)
    x1 = _mix(attn, proj, x2, p["pool_w"].astype(BF16), p["pool_scale"][None, :],
              p["w_attn_up"].astype(BF16), p["w_pool_up"].astype(BF16), p["w_mix_out"].astype(BF16), seq)

    kmem, vmem = _memkv(mem, p["mem_norm_g"][None, :], p["w_ck"].astype(BF16), p["w_cv"].astype(BF16))
    w_router_pad = jnp.pad(p["w_router"], ((0, 0), (0, LANES - N_EXPERTS)))
    b_router_pad = jnp.pad(p["b_router"], (0, LANES - N_EXPERTS))[None, :]
    x2b, h_packed, logits_t = _cross(x1, p["cross_norm_g"][None, :], p["w_cq"].astype(BF16), kmem, vmem,
                                     p["w_co"].astype(BF16), p["moe_norm_g"][None, :], w_router_pad,
                                     b_router_pad, seq)

    pos, gates, counts, nblk, start = _route(logits_t)
    counts, nblk, start = counts[:, 0], nblk[:, 0], start[:, 0]
    pos_flat = pos.T.reshape(-1)
    n_blocks = (n * TOP_K) // SLOT_BLK + N_EXPERTS
    max_items = (n * TOP_K) // (SLOT_BLK * SUB_PER_ITEM) + N_EXPERTS
    xs = _dispatch(pos_flat, counts, nblk, start, h_packed, n_blocks * SLOT_BLK)
    item_e, item_b, item_n, n_items = _work_items(nblk, start, max_items)
    ys = _experts(item_e, item_b, item_n, n_items, xs, p["w_gate_up"], p["b_gate_up"][:, None, :],
                  p["w_down"], p["b_down"][:, None, :], n_blocks, max_items)
    return pos_flat, ys, gates.T, x2b


def kernel(x, mem, mix_norm_g, w_in, q_norm_g, k_norm_g, pool_w, pool_scale, w_attn_up, w_pool_up, w_mix_out,
           cross_norm_g, mem_norm_g, w_cq, w_ck, w_cv, w_co, moe_norm_g, w_router, b_router, w_gate_up,
           b_gate_up, w_down, b_down, final_norm_g):
    batch, seq, d = x.shape
    depth = w_in.shape[0]
    assert depth == 1, "the combine stage applies the final norm, so exactly one layer is supported"
    cos, sin_signed = _rope_tables(seq)
    x2 = x.reshape(batch * seq, d)
    p = dict(mix_norm_g=mix_norm_g[0], w_in=w_in[0], q_norm_g=q_norm_g[0], k_norm_g=k_norm_g[0],
             pool_w=pool_w[0], pool_scale=pool_scale[0], w_attn_up=w_attn_up[0], w_pool_up=w_pool_up[0],
             w_mix_out=w_mix_out[0], cross_norm_g=cross_norm_g[0], mem_norm_g=mem_norm_g[0], w_cq=w_cq[0],
             w_ck=w_ck[0], w_cv=w_cv[0], w_co=w_co[0], moe_norm_g=moe_norm_g[0], w_router=w_router[0],
             b_router=b_router[0], w_gate_up=w_gate_up[0], b_gate_up=b_gate_up[0], w_down=w_down[0],
             b_down=b_down[0])
    pos_flat, ys, gates, x2b = _layer(x2, mem, p, batch, seq, cos, sin_signed)
    out = _combine(pos_flat, ys, gates, x2b, final_norm_g[None, :])
    return out.reshape(batch, seq, d)
```

```python
import functools

import jax
import jax.numpy as jnp
from jax import lax
from jax.experimental import pallas as pl
from jax.experimental.pallas import tpu as pltpu

F32 = jnp.float32
BF16 = jnp.bfloat16

D_MODEL = 2048
GRID_W = 64
HEAD_DIM = 128
N_Q_HEADS = 8
N_KV_HEADS = 2
Q_PER_KV = N_Q_HEADS // N_KV_HEADS
ATTN_WIDTH = N_Q_HEADS * HEAD_DIM
KV_WIDTH = N_KV_HEADS * HEAD_DIM
ROPE_THETA = 10000.0
POOL_WINDOWS = (2, 4, 8, 16)
N_POOL_GROUPS = len(POOL_WINDOWS)
POOL_WIDTH = D_MODEL // 2
POOL_GROUP = POOL_WIDTH // N_POOL_GROUPS
Q_END = ATTN_WIDTH
K_END = Q_END + KV_WIDTH
V_END = K_END + KV_WIDTH
P_END = V_END + POOL_WIDTH
GA_END = P_END + D_MODEL
IN_WIDTH = GA_END + D_MODEL
N_MEM_HEADS = 4
MEM_HEAD_DIM = 128
MEM_WIDTH = N_MEM_HEADS * MEM_HEAD_DIM
N_EXPERTS = 32
TOP_K = 4
D_EXPERT = D_MODEL
SWIGLU_LIMIT = 7.0
SWIGLU_ALPHA = 1.702
NORM_EPS = 1e-6

LANES = 128
BF16_ROWS = 16
VMEM_LIMIT = 56 * 1024 * 1024

PJ_GA = 0
PJ_GP = D_MODEL
PJ_U = 2 * D_MODEL
PJ_Q = PJ_U + POOL_WIDTH
PJ_K = PJ_Q + ATTN_WIDTH
PJ_V = PJ_K + KV_WIDTH

PROJ_TM = 1024
PROJ_TN = 512
ATTN_TQ = 512
ATTN_TK = 1024
MIX_TM = 256
HALO = BF16_ROWS
CROSS_TM = 256
ROUTE_CHUNK = 256
SLOT_BLK = 256
SUB_PER_ITEM = 4
FFN_CHUNK = 256
DISPATCH_CHUNK = 512
COMBINE_TM = 256


def _cparams(sem, vmem=VMEM_LIMIT):
    return pltpu.CompilerParams(dimension_semantics=sem, vmem_limit_bytes=vmem)


def _rms(x, g):
    return x * lax.rsqrt(jnp.mean(x * x, axis=-1, keepdims=True) + NORM_EPS) * g


def _proj_kernel(x_ref, g_ref, w_ref, o_ref, h_ref, *, n_gate_blocks):
    j = pl.program_id(1)

    @pl.when(j == 0)
    def _():
        h_ref[...] = _rms(x_ref[...], g_ref[...]).astype(BF16)

    acc = jnp.dot(h_ref[...], w_ref[...], preferred_element_type=F32)
    o_ref[...] = jnp.where(j < n_gate_blocks, 1.0 / (1.0 + jnp.exp(-acc)), acc).astype(BF16)


def _proj(x2, g, w):
    n, d = x2.shape
    width = w.shape[1]
    return pl.pallas_call(
        functools.partial(_proj_kernel, n_gate_blocks=PJ_U // PROJ_TN),
        grid=(n // PROJ_TM, width // PROJ_TN),
        in_specs=[
            pl.BlockSpec((PROJ_TM, d), lambda i, j: (i, 0)),
            pl.BlockSpec((1, d), lambda i, j: (0, 0)),
            pl.BlockSpec((d, PROJ_TN), lambda i, j: (0, j)),
        ],
        out_specs=pl.BlockSpec((PROJ_TM, PROJ_TN), lambda i, j: (i, j)),
        out_shape=jax.ShapeDtypeStruct((n, width), BF16),
        scratch_shapes=[pltpu.VMEM((PROJ_TM, d), BF16)],
        compiler_params=_cparams(("parallel", "arbitrary")),
        name="proj",
    )(x2, g, w)


def _rope(x, cos, sin_signed):
    lane = lax.broadcasted_iota(jnp.int32, x.shape, 1)
    low = (lane % (HEAD_DIM // 2)) < (HEAD_DIM // 4)
    partner = jnp.where(low, pltpu.roll(x, HEAD_DIM - HEAD_DIM // 4, 1), pltpu.roll(x, HEAD_DIM // 4, 1))
    return x * cos + partner * sin_signed


def _attn_kernel(q_ref, k_ref, v_ref, cos_ref, sin_ref, qg_ref, kg_ref, o_ref, kp_ref, *, tq):
    h = pl.program_id(1)
    qi = pl.program_id(2)

    @pl.when((qi == 0) & (h % Q_PER_KV == 0))
    def _():
        kn = _rms(k_ref[...].astype(F32), kg_ref[...])
        kp_ref[...] = _rope(kn, cos_ref[...], sin_ref[...]).astype(BF16)

    rows = pl.ds(pl.multiple_of(qi * tq, tq), tq)
    qn = _rms(q_ref[...].astype(F32), qg_ref[...])
    qr = (_rope(qn, cos_ref[rows, :], sin_ref[rows, :]) * (HEAD_DIM ** -0.5)).astype(BF16)
    m = l = o = None
    for c in range(kp_ref.shape[0] // ATTN_TK):
        ks = pl.ds(c * ATTN_TK, ATTN_TK)
        s = lax.dot_general(qr, kp_ref[ks, :], (((1,), (1,)), ((), ())), preferred_element_type=F32)
        mc = jnp.max(s, axis=-1, keepdims=True)
        if c == 0:
            m = mc
            p = jnp.exp(s - m)
            l = jnp.sum(p, axis=-1, keepdims=True)
            o = jnp.dot(p.astype(BF16), v_ref[ks, :], preferred_element_type=F32)
        else:
            m_new = jnp.maximum(m, mc)
            alpha = jnp.exp(m - m_new)
            p = jnp.exp(s - m_new)
            l = alpha * l + jnp.sum(p, axis=-1, keepdims=True)
            o = alpha * o + jnp.dot(p.astype(BF16), v_ref[ks, :], preferred_element_type=F32)
            m = m_new
    o_ref[...] = (o / l).astype(BF16)


def _attention(proj, cos, sin_signed, q_g, k_g, batch, seq):
    tq = ATTN_TQ
    nq = seq // tq
    qc, kc, vc = PJ_Q // HEAD_DIM, PJ_K // HEAD_DIM, PJ_V // HEAD_DIM
    return pl.pallas_call(
        functools.partial(_attn_kernel, tq=tq),
        grid=(batch, N_Q_HEADS, nq),
        in_specs=[
            pl.BlockSpec((tq, HEAD_DIM), lambda b, h, i: (b * nq + i, qc + h)),
            pl.BlockSpec((seq, HEAD_DIM), lambda b, h, i: (b, kc + h // Q_PER_KV)),
            pl.BlockSpec((seq, HEAD_DIM), lambda b, h, i: (b, vc + h // Q_PER_KV)),
            pl.BlockSpec((seq, HEAD_DIM), lambda b, h, i: (0, 0)),
            pl.BlockSpec((seq, HEAD_DIM), lambda b, h, i: (0, 0)),
            pl.BlockSpec((1, HEAD_DIM), lambda b, h, i: (0, 0)),
            pl.BlockSpec((1, HEAD_DIM), lambda b, h, i: (0, 0)),
        ],
        out_specs=pl.BlockSpec((tq, HEAD_DIM), lambda b, h, i: (b * nq + i, h)),
        out_shape=jax.ShapeDtypeStruct((batch * seq, ATTN_WIDTH), BF16),
        scratch_shapes=[pltpu.VMEM((seq, HEAD_DIM), BF16)],
        compiler_params=_cparams(("arbitrary", "arbitrary", "arbitrary")),
        name="attn",
    )(proj, proj, proj, cos, sin_signed, q_g, k_g)


def _mix_kernel(attn_ref, u_ref, up_ref, un_ref, ga_ref, gp_ref, x_ref, pw_ref, ps_ref, wa_ref, wp_ref,
                wo_ref, o_ref, *, tm, seq):
    i = pl.program_id(0)
    t0 = (i * tm) % seq
    u = u_ref[...]
    prev = up_ref[...]
    nxt = un_ref[...]
    has_prev = t0 > 0
    has_next = t0 + tm < seq
    r = lax.broadcasted_iota(jnp.int32, (tm, tm), 0)
    c = lax.broadcasted_iota(jnp.int32, (tm, tm), 1)
    rh = lax.broadcasted_iota(jnp.int32, (tm, HALO), 0)
    ch = lax.broadcasted_iota(jnp.int32, (tm, HALO), 1)
    tpos = t0 + lax.broadcasted_iota(jnp.int32, (tm, 1), 0)
    ys = []
    for gi, w in enumerate(POOL_WINDOWS):
        sl = slice(gi * POOL_GROUP, (gi + 1) * POOL_GROUP)
        lo, hi = w // 2, w - w // 2
        band = jnp.where((c >= r - lo) & (c < r + hi), 1.0, 0.0).astype(BF16)
        band_p = jnp.where(has_prev & (ch - HALO >= rh - lo), 1.0, 0.0).astype(BF16)
        band_n = jnp.where(has_next & (ch + tm < rh + hi), 1.0, 0.0).astype(BF16)
        sums = (jnp.dot(band, u[:, sl], preferred_element_type=F32)
                + jnp.dot(band_p, prev[:, sl], preferred_element_type=F32)
                + jnp.dot(band_n, nxt[:, sl], preferred_element_type=F32))
        cnt = jnp.minimum(tpos + (w - w // 2), seq) - jnp.maximum(tpos - w // 2, 0)
        pooled = sums / cnt.astype(F32) - u[:, sl].astype(F32)
        ys.append(jnp.dot(pooled.astype(BF16), pw_ref[gi], preferred_element_type=F32))
    pool = (jnp.concatenate(ys, axis=1) * ps_ref[...]).astype(BF16)
    lift_a = jnp.dot(attn_ref[...], wa_ref[...], preferred_element_type=F32)
    lift_p = jnp.dot(pool, wp_ref[...], preferred_element_type=F32)
    mixed = ga_ref[...].astype(F32) * lift_a + gp_ref[...].astype(F32) * lift_p
    o_ref[...] = x_ref[...] + jnp.dot(mixed.astype(BF16), wo_ref[...], preferred_element_type=F32)


def _resident(shape):
    nd = len(shape)
    return pl.BlockSpec(shape, lambda *_: (0,) * nd, pipeline_mode=pl.Buffered(1))


def _mix(attn, proj, x2, pool_w, pool_scale, w_attn_up, w_pool_up, w_mix_out, seq):
    n = x2.shape[0]
    tm = MIX_TM
    hb = tm // HALO
    last_halo = n // HALO - 1
    return pl.pallas_call(
        functools.partial(_mix_kernel, tm=tm, seq=seq),
        grid=(n // tm,),
        in_specs=[
            pl.BlockSpec((tm, ATTN_WIDTH), lambda i: (i, 0)),
            pl.BlockSpec((tm, POOL_WIDTH), lambda i: (i, PJ_U // POOL_WIDTH)),
            pl.BlockSpec((HALO, POOL_WIDTH), lambda i: (jnp.maximum(i * hb - 1, 0), PJ_U // POOL_WIDTH)),
            pl.BlockSpec((HALO, POOL_WIDTH), lambda i: (jnp.minimum((i + 1) * hb, last_halo), PJ_U // POOL_WIDTH)),
            pl.BlockSpec((tm, D_MODEL), lambda i: (i, PJ_GA // D_MODEL)),
            pl.BlockSpec((tm, D_MODEL), lambda i: (i, PJ_GP // D_MODEL)),
            pl.BlockSpec((tm, D_MODEL), lambda i: (i, 0)),
            _resident(pool_w.shape),
            _resident(pool_scale.shape),
            _resident(w_attn_up.shape),
            _resident(w_pool_up.shape),
            _resident(w_mix_out.shape),
        ],
        out_specs=pl.BlockSpec((tm, D_MODEL), lambda i: (i, 0)),
        out_shape=jax.ShapeDtypeStruct((n, D_MODEL), F32),
        compiler_params=_cparams(("parallel",)),
        name="mix",
    )(attn, proj, proj, proj, proj, proj, x2, pool_w, pool_scale, w_attn_up, w_pool_up, w_mix_out)


def _memkv_kernel(m_ref, g_ref, wk_ref, wv_ref, k_ref, v_ref):
    m = _rms(m_ref[0], g_ref[...]).astype(BF16)
    k_ref[0] = jnp.dot(m, wk_ref[...], preferred_element_type=F32).astype(BF16)
    v_ref[0] = jnp.dot(m, wv_ref[...], preferred_element_type=F32).astype(BF16)


def _memkv(mem, g, w_ck, w_cv):
    b, m, d = mem.shape
    out = jax.ShapeDtypeStruct((b, m, MEM_WIDTH), BF16)
    return pl.pallas_call(
        _memkv_kernel,
        grid=(b,),
        in_specs=[
            pl.BlockSpec((1, m, d), lambda i: (i, 0, 0)),
            pl.BlockSpec((1, d), lambda i: (0, 0)),
            pl.BlockSpec((d, MEM_WIDTH), lambda i: (0, 0)),
            pl.BlockSpec((d, MEM_WIDTH), lambda i: (0, 0)),
        ],
        out_specs=[pl.BlockSpec((1, m, MEM_WIDTH), lambda i: (i, 0, 0))] * 2,
        out_shape=[out, out],
        compiler_params=_cparams(("parallel",)),
        name="memkv",
    )(mem, g, w_ck, w_cv)


def _split_bf16(a):
    hi = a.astype(BF16)
    lo = (a - hi.astype(F32)).astype(BF16)
    return hi, lo


def _cross_kernel(x_ref, cg_ref, wq_ref, k_ref, v_ref, wo_ref, mg_ref, wr_ref, br_ref,
                  x2_ref, hp_ref, lt_ref):
    x = x_ref[...]
    h = _rms(x, cg_ref[...]).astype(BF16)
    q = jnp.dot(h, wq_ref[...], preferred_element_type=F32)
    k = k_ref[0]
    v = v_ref[0]
    outs = []
    for hd in range(N_MEM_HEADS):
        sl = slice(hd * MEM_HEAD_DIM, (hd + 1) * MEM_HEAD_DIM)
        s = lax.dot_general(q[:, sl].astype(BF16), k[:, sl], (((1,), (1,)), ((), ())),
                            preferred_element_type=F32) * (MEM_HEAD_DIM ** -0.5)
        p = jnp.exp(s - jnp.max(s, axis=-1, keepdims=True))
        l = jnp.sum(p, axis=-1, keepdims=True)
        outs.append(jnp.dot(p.astype(BF16), v[:, sl], preferred_element_type=F32) / l)
    o = jnp.concatenate(outs, axis=1).astype(BF16)
    x2 = x + jnp.dot(o, wo_ref[...], preferred_element_type=F32)
    x2_ref[...] = x2

    h3 = _rms(x2, mg_ref[...])
    h_hi, h_lo = _split_bf16(h3)
    half = D_MODEL // 2
    bits = pltpu.bitcast(h_hi.astype(F32), jnp.uint32)
    hp_ref[...] = (bits[:, :half] >> 16) | (bits[:, half:] & jnp.uint32(0xFFFF0000))

    w_hi, w_lo = _split_bf16(wr_ref[...])
    logits = (jnp.dot(h_hi, w_hi, preferred_element_type=F32)
              + jnp.dot(h_hi, w_lo, preferred_element_type=F32)
              + jnp.dot(h_lo, w_hi, preferred_element_type=F32)) + br_ref[...]
    lt_ref[...] = logits.T[:N_EXPERTS, :]


def _cross(x1, cross_g, w_cq, kmem, vmem, w_co, moe_g, w_router_pad, b_router_pad, seq):
    n = x1.shape[0]
    tm = CROSS_TM
    per_seq = seq // tm
    mlen = kmem.shape[1]
    return pl.pallas_call(
        _cross_kernel,
        grid=(n // tm,),
        in_specs=[
            pl.BlockSpec((tm, D_MODEL), lambda i: (i, 0)),
            _resident((1, D_MODEL)),
            _resident(w_cq.shape),
            pl.BlockSpec((1, mlen, MEM_WIDTH), lambda i: (i // per_seq, 0, 0)),
            pl.BlockSpec((1, mlen, MEM_WIDTH), lambda i: (i // per_seq, 0, 0)),
            _resident(w_co.shape),
            _resident((1, D_MODEL)),
            _resident(w_router_pad.shape),
            _resident(b_router_pad.shape),
        ],
        out_specs=[
            pl.BlockSpec((tm, D_MODEL), lambda i: (i, 0)),
            pl.BlockSpec((tm, D_MODEL // 2), lambda i: (i, 0)),
            pl.BlockSpec((N_EXPERTS, tm), lambda i: (0, i)),
        ],
        out_shape=[
            jax.ShapeDtypeStruct((n, D_MODEL), F32),
            jax.ShapeDtypeStruct((n, D_MODEL // 2), jnp.uint32),
            jax.ShapeDtypeStruct((N_EXPERTS, n), F32),
        ],
        compiler_params=_cparams(("parallel",)),
        name="cross",
    )(x1, cross_g, w_cq, kmem, vmem, w_co, moe_g, w_router_pad, b_router_pad)


def _route_kernel(lt_ref, pos_ref, gate_ref, cnt_ref, nblk_ref, start_ref, m_ref, rank_ref):
    n = lt_ref.shape[1]
    l = lt_ref[...]
    row = lax.broadcasted_iota(jnp.int32, l.shape, 0)
    sels, vals = [], []
    for _ in range(TOP_K):
        m = jnp.max(l, axis=0, keepdims=True)
        idx = jnp.min(jnp.where(l == m, row, N_EXPERTS), axis=0, keepdims=True)
        sel = row == idx
        sels.append(sel)
        vals.append(m)
        l = jnp.where(sel, -jnp.inf, l)

    es = [jnp.exp(v - vals[0]) for v in vals]
    den = es[0] + es[1] + es[2] + es[3]
    for k in range(TOP_K):
        gate_ref[k:k + 1, :] = es[k] / den

    chosen = sels[0] | sels[1] | sels[2] | sels[3]
    m_ref[...] = jnp.where(chosen, 1.0, 0.0)

    ci = lax.broadcasted_iota(jnp.int32, (ROUTE_CHUNK, ROUTE_CHUNK), 0)
    cj = lax.broadcasted_iota(jnp.int32, (ROUTE_CHUNK, ROUTE_CHUNK), 1)
    upper = jnp.where(ci < cj, 1.0, 0.0).astype(BF16)

    def chunk(c, carry):
        cols = pl.ds(pl.multiple_of(c * ROUTE_CHUNK, ROUTE_CHUNK), ROUTE_CHUNK)
        mt = m_ref[:, cols]
        rank_ref[:, cols] = jnp.dot(mt.astype(BF16), upper, preferred_element_type=F32) + carry
        return carry + jnp.sum(mt, axis=1, keepdims=True)

    counts = lax.fori_loop(0, n // ROUTE_CHUNK, chunk, jnp.zeros((N_EXPERTS, 1), F32))
    nblk = jnp.floor((counts + (SLOT_BLK - 1)) * (1.0 / SLOT_BLK))
    ei = lax.broadcasted_iota(jnp.int32, (N_EXPERTS, N_EXPERTS), 0)
    ej = lax.broadcasted_iota(jnp.int32, (N_EXPERTS, N_EXPERTS), 1)
    lower = jnp.where(ej < ei, 1.0, 0.0).astype(BF16)
    nblk_b = jnp.broadcast_to(nblk, (N_EXPERTS, LANES))
    start_b = jnp.dot(lower, nblk_b.astype(BF16), preferred_element_type=F32)
    cnt_ref[...] = jnp.broadcast_to(counts, (N_EXPERTS, LANES)).astype(jnp.int32)
    nblk_ref[...] = nblk_b.astype(jnp.int32)
    start_ref[...] = start_b.astype(jnp.int32)

    slot = start_b[:, :1] * SLOT_BLK + rank_ref[...]
    for k in range(TOP_K):
        pos_ref[k:k + 1, :] = jnp.sum(jnp.where(sels[k], slot, 0.0), axis=0, keepdims=True).astype(jnp.int32)


def _route(logits_t):
    e, n = logits_t.shape
    small = jax.ShapeDtypeStruct((e, LANES), jnp.int32)
    return pl.pallas_call(
        _route_kernel,
        out_shape=[
            jax.ShapeDtypeStruct((TOP_K, n), jnp.int32),
            jax.ShapeDtypeStruct((TOP_K, n), F32),
            small, small, small,
        ],
        scratch_shapes=[pltpu.VMEM((e, n), F32), pltpu.VMEM((e, n), F32)],
        compiler_params=pltpu.CompilerParams(vmem_limit_bytes=VMEM_LIMIT),
        name="route",
    )(logits_t)


def _dispatch_kernel(pos_ref, cnt_ref, nblk_ref, start_ref, h_ref, xs_ref, zero_ref, sem, pad_sem, *, tm):
    i = pl.program_id(0)

    def tok(r, _):
        t = i * tm + r
        for k in range(TOP_K):
            p = pos_ref[t * TOP_K + k]
            pltpu.make_async_copy(h_ref.at[pl.ds(r, 1)], xs_ref.at[pl.ds(p, 1)], sem).start()
        return 0

    lax.fori_loop(0, tm, tok, 0)
    for _ in range(TOP_K):
        pltpu.make_async_copy(h_ref, xs_ref.at[pl.ds(0, tm)], sem).wait()

    @pl.when(i == pl.num_programs(0) - 1)
    def _():
        _dispatch_fill(cnt_ref, nblk_ref, start_ref, xs_ref, zero_ref, pad_sem)


def _dispatch_fill(cnt_ref, nblk_ref, start_ref, xs_ref, zero_ref, pad_sem):
    zero_ref[...] = jnp.zeros(zero_ref.shape, zero_ref.dtype)

    def pad_expert(e, _):
        lo = start_ref[e] * SLOT_BLK + cnt_ref[e]
        hi = (start_ref[e] + nblk_ref[e]) * SLOT_BLK

        def fill(r, _):
            pltpu.make_async_copy(zero_ref.at[pl.ds(0, 1)], xs_ref.at[pl.ds(r, 1)], pad_sem).start()
            return 0

        def drain(r, _):
            pltpu.make_async_copy(zero_ref.at[pl.ds(0, 1)], xs_ref.at[pl.ds(r, 1)], pad_sem).wait()
            return 0

        lax.fori_loop(lo, hi, fill, 0)
        lax.fori_loop(lo, hi, drain, 0)
        return 0

    lax.fori_loop(0, N_EXPERTS, pad_expert, 0)

    def tail(b, _):
        rows = pl.ds(pl.multiple_of(b * SLOT_BLK, SLOT_BLK), SLOT_BLK)
        cp = pltpu.make_async_copy(zero_ref, xs_ref.at[rows], pad_sem)
        cp.start()
        cp.wait()
        return 0

    lax.fori_loop(start_ref[N_EXPERTS - 1] + nblk_ref[N_EXPERTS - 1], xs_ref.shape[0] // SLOT_BLK, tail, 0)


def _dispatch(pos_flat, counts, nblk, start, h_packed, n_slots):
    n_tok, width = h_packed.shape
    tm = DISPATCH_CHUNK
    return pl.pallas_call(
        functools.partial(_dispatch_kernel, tm=tm),
        grid_spec=pltpu.PrefetchScalarGridSpec(
            num_scalar_prefetch=4,
            grid=(n_tok // tm,),
            in_specs=[pl.BlockSpec((tm, width), lambda i, *_: (i, 0))],
            out_specs=pl.BlockSpec(memory_space=pl.ANY),
            scratch_shapes=[
                pltpu.VMEM((SLOT_BLK, width), h_packed.dtype),
                pltpu.SemaphoreType.DMA(()),
                pltpu.SemaphoreType.DMA(()),
            ],
        ),
        out_shape=jax.ShapeDtypeStruct((n_slots, width), h_packed.dtype),
        compiler_params=pltpu.CompilerParams(dimension_semantics=("arbitrary",)),
        name="dispatch",
    )(pos_flat, counts, nblk, start, h_packed)


def _swiglu_pairs(h_a, h_b):
    lane = lax.broadcasted_iota(jnp.int32, h_a.shape, 1)
    even = (lane % 2) == 0
    gate_in = jnp.where(even, h_a, pltpu.roll(h_b, 1, 1))
    lin_in = jnp.where(even, pltpu.roll(h_a, LANES - 1, 1), h_b)
    glu = jnp.minimum(gate_in, SWIGLU_LIMIT)
    lin = jnp.clip(lin_in, -SWIGLU_LIMIT, SWIGLU_LIMIT)
    return glu * (1.0 / (1.0 + jnp.exp(-SWIGLU_ALPHA * glu))) * (lin + 1.0)


def _experts_kernel(ie_ref, ib_ref, ins_ref, ni_ref, *refs):
    x_refs = refs[:SUB_PER_ITEM]
    w1_ref, b1_ref, w2_ref, b2_ref, ys_ref, xb_ref, w2q_ref, acc_ref, sem = refs[SUB_PER_ITEM:]
    w = pl.program_id(0)
    j = pl.program_id(1)
    last_j = pl.num_programs(1) - 1
    n_items = ni_ref[0]
    nsub = ins_ref[w]
    slot = w % 2
    half = D_MODEL // 2
    quarter = FFN_CHUNK // 4

    def sub_rows(s):
        return pl.ds(s * SLOT_BLK, SLOT_BLK)

    def writeback(item, item_slot, s):
        rows = pl.ds(pl.multiple_of((ib_ref[item] + s) * SLOT_BLK, SLOT_BLK), SLOT_BLK)
        return pltpu.make_async_copy(acc_ref.at[item_slot, sub_rows(s)], ys_ref.at[rows], sem.at[item_slot])

    def wait_item(item, item_slot):
        for s in range(SUB_PER_ITEM):
            @pl.when(s < ins_ref[item])
            def _(s=s):
                writeback(item, item_slot, s).wait()

    def ffn_chunk(n):
        rows = n * SLOT_BLK
        w1 = w1_ref[...].astype(BF16)
        for dst, src in ((0, 0), (1, 2 * quarter), (2 * quarter, quarter), (2 * quarter + 1, 3 * quarter)):
            for c in range(D_MODEL // LANES):
                w2q_ref[c, pl.ds(dst, quarter, stride=2), :] = w2_ref[pl.ds(src, quarter), c * LANES:(c + 1) * LANES]
        w2 = jnp.concatenate([w2q_ref[c] for c in range(D_MODEL // LANES)], axis=1).astype(BF16)
        h = jnp.dot(xb_ref[0:rows, :], w1, preferred_element_type=F32) + b1_ref[...]
        act = jnp.concatenate(
            [_swiglu_pairs(h[:, 0:LANES], h[:, 2 * LANES:3 * LANES]),
             _swiglu_pairs(h[:, LANES:2 * LANES], h[:, 3 * LANES:4 * LANES])], axis=1)
        acc_ref[slot, 0:rows, :] += jnp.dot(act.astype(BF16), w2, preferred_element_type=F32)

    @pl.when(w < n_items)
    def _():
        @pl.when(j == 0)
        def _():
            @pl.when(w >= 2)
            def _():
                wait_item(w - 2, slot)

            for s in range(SUB_PER_ITEM):
                @pl.when(s < nsub)
                def _(s=s):
                    word = x_refs[s][...]
                    xb_ref[sub_rows(s), :half] = pltpu.bitcast(word << 16, F32).astype(BF16)
                    xb_ref[sub_rows(s), half:] = pltpu.bitcast(word & jnp.uint32(0xFFFF0000), F32).astype(BF16)
                    acc_ref[slot, sub_rows(s), :] = jnp.broadcast_to(b2_ref[...], (SLOT_BLK, D_MODEL))

        for n in range(1, SUB_PER_ITEM + 1):
            @pl.when(nsub == n)
            def _(n=n):
                ffn_chunk(n)

        @pl.when(j == last_j)
        def _():
            for s in range(SUB_PER_ITEM):
                @pl.when(s < nsub)
                def _(s=s):
                    writeback(w, slot, s).start()

        @pl.when((j == last_j) & (w == n_items - 1))
        def _():
            wait_item(w, slot)

            @pl.when(w >= 1)
            def _():
                wait_item(w - 1, 1 - slot)

            zero_blk = acc_ref.at[slot, sub_rows(0)]
            zero_blk[...] = jnp.zeros(zero_blk.shape, F32)

            def tail(b, _):
                rows = pl.ds(pl.multiple_of(b * SLOT_BLK, SLOT_BLK), SLOT_BLK)
                cp = pltpu.make_async_copy(zero_blk, ys_ref.at[rows], sem.at[slot])
                cp.start()
                cp.wait()
                return 0

            lax.fori_loop(ib_ref[w] + nsub, ys_ref.shape[0] // SLOT_BLK, tail, 0)


def _experts(item_e, item_b, item_n, n_items, xs, w_gate_up, b_gate_up, w_down, b_down, n_blocks, max_items):
    width = xs.shape[1]
    n_chunks = D_EXPERT // FFN_CHUNK

    def x_spec(s):
        return pl.BlockSpec((SLOT_BLK, width),
                            lambda w, j, ie, ib, ins, ni: (ib[w] + jnp.minimum(s, ins[w] - 1), 0))

    in_specs = [x_spec(s) for s in range(SUB_PER_ITEM)] + [
        pl.BlockSpec((None, D_MODEL, 2 * FFN_CHUNK), lambda w, j, ie, ib, ins, ni: (ie[w], 0, j)),
        pl.BlockSpec((None, 1, 2 * FFN_CHUNK), lambda w, j, ie, ib, ins, ni: (ie[w], 0, j)),
        pl.BlockSpec((None, FFN_CHUNK, D_MODEL), lambda w, j, ie, ib, ins, ni: (ie[w], j, 0)),
        pl.BlockSpec((None, 1, D_MODEL), lambda w, j, ie, ib, ins, ni: (ie[w], 0, 0)),
    ]
    return pl.pallas_call(
        _experts_kernel,
        grid_spec=pltpu.PrefetchScalarGridSpec(
            num_scalar_prefetch=4,
            grid=(max_items, n_chunks),
            in_specs=in_specs,
            out_specs=pl.BlockSpec(memory_space=pl.ANY),
            scratch_shapes=[
                pltpu.VMEM((SUB_PER_ITEM * SLOT_BLK, D_MODEL), BF16),
                pltpu.VMEM((D_MODEL // LANES, FFN_CHUNK, LANES), F32),
                pltpu.VMEM((2, SUB_PER_ITEM * SLOT_BLK, D_MODEL), F32),
                pltpu.SemaphoreType.DMA((2,)),
            ],
        ),
        out_shape=jax.ShapeDtypeStruct((n_blocks * SLOT_BLK, D_MODEL), F32),
        compiler_params=_cparams(("arbitrary", "arbitrary")),
        name="experts",
    )(item_e, item_b, item_n, n_items, *([xs] * SUB_PER_ITEM), w_gate_up, b_gate_up, w_down, b_down)


def _combine_kernel(pos_ref, ys_ref, gate_ref, x_ref, g_ref, o_ref, buf_ref, sem, *, tm):
    i = pl.program_id(0)
    n_steps = pl.num_programs(0)

    def issue(step, slot):
        def tok(r, _):
            t = step * tm + r
            for k in range(TOP_K):
                p = pos_ref[t * TOP_K + k]
                pltpu.make_async_copy(ys_ref.at[pl.ds(p, 1)], buf_ref.at[slot, k, pl.ds(r, 1)], sem.at[slot]).start()
            return 0
        lax.fori_loop(0, tm, tok, 0)

    @pl.when(i == 0)
    def _():
        issue(0, 0)

    @pl.when(i + 1 < n_steps)
    def _():
        issue(i + 1, (i + 1) % 2)

    slot = i % 2
    for k in range(TOP_K):
        pltpu.make_async_copy(ys_ref.at[pl.ds(0, tm)], buf_ref.at[slot, k], sem.at[slot]).wait()
    gates = gate_ref[...]
    acc = x_ref[...]
    for k in range(TOP_K):
        acc = acc + gates[:, k:k + 1] * buf_ref[slot, k]
    o_ref[...] = _rms(acc, g_ref[...])


def _combine(pos_flat, ys, gates, x2, final_g):
    n = x2.shape[0]
    tm = COMBINE_TM
    return pl.pallas_call(
        functools.partial(_combine_kernel, tm=tm),
        grid_spec=pltpu.PrefetchScalarGridSpec(
            num_scalar_prefetch=1,
            grid=(n // tm,),
            in_specs=[
                pl.BlockSpec(memory_space=pl.ANY),
                pl.BlockSpec((tm, TOP_K), lambda i, pos: (i, 0)),
                pl.BlockSpec((tm, D_MODEL), lambda i, pos: (i, 0)),
                pl.BlockSpec((1, D_MODEL), lambda i, pos: (0, 0)),
            ],
            out_specs=pl.BlockSpec((tm, D_MODEL), lambda i, pos: (i, 0)),
            scratch_shapes=[
                pltpu.VMEM((2, TOP_K, tm, D_MODEL), F32),
                pltpu.SemaphoreType.DMA((2,)),
            ],
        ),
        out_shape=jax.ShapeDtypeStruct((n, D_MODEL), F32),
        compiler_params=_cparams(("arbitrary",)),
        name="combine",
    )(pos_flat, ys, gates, x2, final_g)


def _rope_tables(seq):
    t = jnp.arange(seq, dtype=jnp.int32)
    rows = (t // GRID_W).astype(F32)
    cols = (t % GRID_W).astype(F32)
    axis_dim = HEAD_DIM // 2
    inv_freq = ROPE_THETA ** (-jnp.arange(0, axis_dim, 2, dtype=F32) / axis_dim)
    ar = rows[:, None] * inv_freq
    ac = cols[:, None] * inv_freq
    cos = jnp.concatenate([jnp.cos(ar), jnp.cos(ar), jnp.cos(ac), jnp.cos(ac)], axis=1)
    sin = jnp.concatenate([-jnp.sin(ar), jnp.sin(ar), -jnp.sin(ac), jnp.sin(ac)], axis=1)
    return cos, sin


def _work_items(nblk, start, max_items):
    per_e = (nblk + SUB_PER_ITEM - 1) // SUB_PER_ITEM
    ends = jnp.cumsum(per_e)
    n_items = ends[-1]
    w = jnp.minimum(jnp.arange(max_items, dtype=jnp.int32), n_items - 1)
    e = jnp.minimum(jnp.searchsorted(ends, w, side="right"), N_EXPERTS - 1).astype(jnp.int32)
    local = w - (ends - per_e)[e]
    blk0 = start[e] + local * SUB_PER_ITEM
    nsub = jnp.clip(nblk[e] - local * SUB_PER_ITEM, 1, SUB_PER_ITEM)
    return e, blk0.astype(jnp.int32), nsub.astype(jnp.int32), n_items.reshape(1).astype(jnp.int32)


def _layer(x2, mem, p, batch, seq, cos, sin_signed):
    n = x2.shape[0]
    w_in = p["w_in"]
    w_in_r = jnp.concatenate(
        [w_in[:, P_END:GA_END], w_in[:, GA_END:], w_in[:, V_END:P_END], w_in[:, :V_END]], axis=1).astype(BF16)
    proj = _proj(x2, p["mix_norm_g"][None, :], w_in_r)
    attn = _attention(proj, cos, sin_signed, p["q_norm_g"][None, :], p["k_norm_g"][None, :], batch, seq)
    x1 = _mix(attn, proj, x2, p["pool_w"].astype(BF16), p["pool_scale"][None, :],
              p["w_attn_up"].astype(BF16), p["w_pool_up"].astype(BF16), p["w_mix_out"].astype(BF16), seq)

    kmem, vmem = _memkv(mem, p["mem_norm_g"][None, :], p["w_ck"].astype(BF16), p["w_cv"].astype(BF16))
    w_router_pad = jnp.pad(p["w_router"], ((0, 0), (0, LANES - N_EXPERTS)))
    b_router_pad = jnp.pad(p["b_router"], (0, LANES - N_EXPERTS))[None, :]
    x2b, h_packed, logits_t = _cross(x1, p["cross_norm_g"][None, :], p["w_cq"].astype(BF16), kmem, vmem,
                                     p["w_co"].astype(BF16), p["moe_norm_g"][None, :], w_router_pad,
                                     b_router_pad, seq)

    pos, gates, counts, nblk, start = _route(logits_t)
    counts, nblk, start = counts[:, 0], nblk[:, 0], start[:, 0]
    pos_flat = pos.T.reshape(-1)
    n_blocks = (n * TOP_K) // SLOT_BLK + N_EXPERTS
    max_items = (n * TOP_K) // (SLOT_BLK * SUB_PER_ITEM) + N_EXPERTS
    xs = _dispatch(pos_flat, counts, nblk, start, h_packed, n_blocks * SLOT_BLK)
    item_e, item_b, item_n, n_items = _work_items(nblk, start, max_items)
    ys = _experts(item_e, item_b, item_n, n_items, xs, p["w_gate_up"], p["b_gate_up"][:, None, :],
                  p["w_down"], p["b_down"][:, None, :], n_blocks, max_items)
    return pos_flat, ys, gates.T, x2b


def kernel(x, mem, mix_norm_g, w_in, q_norm_g, k_norm_g, pool_w, pool_scale, w_attn_up, w_pool_up, w_mix_out,
           cross_norm_g, mem_norm_g, w_cq, w_ck, w_cv, w_co, moe_norm_g, w_router, b_router, w_gate_up,
           b_gate_up, w_down, b_down, final_norm_g):
    batch, seq, d = x.shape
    depth = w_in.shape[0]
    assert depth == 1, "the combine stage applies the final norm, so exactly one layer is supported"
    cos, sin_signed = _rope_tables(seq)
    x2 = x.reshape(batch * seq, d)
    p = dict(mix_norm_g=mix_norm_g[0], w_in=w_in[0], q_norm_g=q_norm_g[0], k_norm_g=k_norm_g[0],
             pool_w=pool_w[0], pool_scale=pool_scale[0], w_attn_up=w_attn_up[0], w_pool_up=w_pool_up[0],
             w_mix_out=w_mix_out[0], cross_norm_g=cross_norm_g[0], mem_norm_g=mem_norm_g[0], w_cq=w_cq[0],
             w_ck=w_ck[0], w_cv=w_cv[0], w_co=w_co[0], moe_norm_g=moe_norm_g[0], w_router=w_router[0],
             b_router=b_router[0], w_gate_up=w_gate_up[0], b_gate_up=b_gate_up[0], w_down=w_down[0],
             b_down=b_down[0])
    pos_flat, ys, gates, x2b = _layer(x2, mem, p, batch, seq, cos, sin_signed)
    out = _combine(pos_flat, ys, gates, x2b, final_norm_g[None, :])
    return out.reshape(batch, seq, d)
```

```python
import functools

import jax
import jax.numpy as jnp
from jax import lax
from jax.experimental import pallas as pl
from jax.experimental.pallas import tpu as pltpu

F32 = jnp.float32
BF16 = jnp.bfloat16

D_MODEL = 2048
GRID_W = 64
HEAD_DIM = 128
N_Q_HEADS = 8
N_KV_HEADS = 2
Q_PER_KV = N_Q_HEADS // N_KV_HEADS
ATTN_WIDTH = N_Q_HEADS * HEAD_DIM
KV_WIDTH = N_KV_HEADS * HEAD_DIM
ROPE_THETA = 10000.0
POOL_WINDOWS = (2, 4, 8, 16)
N_POOL_GROUPS = len(POOL_WINDOWS)
POOL_WIDTH = D_MODEL // 2
POOL_GROUP = POOL_WIDTH // N_POOL_GROUPS
Q_END = ATTN_WIDTH
K_END = Q_END + KV_WIDTH
V_END = K_END + KV_WIDTH
P_END = V_END + POOL_WIDTH
GA_END = P_END + D_MODEL
IN_WIDTH = GA_END + D_MODEL
N_MEM_HEADS = 4
MEM_HEAD_DIM = 128
MEM_WIDTH = N_MEM_HEADS * MEM_HEAD_DIM
N_EXPERTS = 32
TOP_K = 4
D_EXPERT = D_MODEL
SWIGLU_LIMIT = 7.0
SWIGLU_ALPHA = 1.702
NORM_EPS = 1e-6

LANES = 128
BF16_ROWS = 16
VMEM_LIMIT = 56 * 1024 * 1024

PJ_GA = 0
PJ_GP = D_MODEL
PJ_U = 2 * D_MODEL
PJ_Q = PJ_U + POOL_WIDTH
PJ_K = PJ_Q + ATTN_WIDTH
PJ_V = PJ_K + KV_WIDTH

PROJ_TM = 1024
PROJ_TN = 512
ATTN_TQ = 512
ATTN_TK = 1024
MIX_TM = 256
HALO = BF16_ROWS
CROSS_TM = 256
ROUTE_CHUNK = 256
SLOT_BLK = 256
SUB_PER_ITEM = 5
FFN_CHUNK = 256
WEIGHT_RING = 3
EXPERTS_VMEM_LIMIT = 62 * 1024 * 1024
DISPATCH_CHUNK = 512
COMBINE_TM = 256


def _cparams(sem, vmem=VMEM_LIMIT):
    return pltpu.CompilerParams(dimension_semantics=sem, vmem_limit_bytes=vmem)


def _rms(x, g):
    return x * lax.rsqrt(jnp.mean(x * x, axis=-1, keepdims=True) + NORM_EPS) * g


def _proj_kernel(x_ref, g_ref, w_ref, o_ref, h_ref, *, n_gate_blocks):
    j = pl.program_id(1)

    @pl.when(j == 0)
    def _():
        h_ref[...] = _rms(x_ref[...], g_ref[...]).astype(BF16)

    acc = jnp.dot(h_ref[...], w_ref[...], preferred_element_type=F32)
    o_ref[...] = jnp.where(j < n_gate_blocks, 1.0 / (1.0 + jnp.exp(-acc)), acc).astype(BF16)


def _proj(x2, g, w):
    n, d = x2.shape
    width = w.shape[1]
    return pl.pallas_call(
        functools.partial(_proj_kernel, n_gate_blocks=PJ_U // PROJ_TN),
        grid=(n // PROJ_TM, width // PROJ_TN),
        in_specs=[
            pl.BlockSpec((PROJ_TM, d), lambda i, j: (i, 0)),
            pl.BlockSpec((1, d), lambda i, j: (0, 0)),
            pl.BlockSpec((d, PROJ_TN), lambda i, j: (0, j)),
        ],
        out_specs=pl.BlockSpec((PROJ_TM, PROJ_TN), lambda i, j: (i, j)),
        out_shape=jax.ShapeDtypeStruct((n, width), BF16),
        scratch_shapes=[pltpu.VMEM((PROJ_TM, d), BF16)],
        compiler_params=_cparams(("parallel", "arbitrary")),
        name="proj",
    )(x2, g, w)


def _rope(x, cos, sin_signed):
    lane = lax.broadcasted_iota(jnp.int32, x.shape, 1)
    low = (lane % (HEAD_DIM // 2)) < (HEAD_DIM // 4)
    partner = jnp.where(low, pltpu.roll(x, HEAD_DIM - HEAD_DIM // 4, 1), pltpu.roll(x, HEAD_DIM // 4, 1))
    return x * cos + partner * sin_signed


def _attn_kernel(q_ref, k_ref, v_ref, cos_ref, sin_ref, qg_ref, kg_ref, o_ref, kp_ref, *, tq):
    h = pl.program_id(1)
    qi = pl.program_id(2)

    @pl.when((qi == 0) & (h % Q_PER_KV == 0))
    def _():
        kn = _rms(k_ref[...].astype(F32), kg_ref[...])
        kp_ref[...] = _rope(kn, cos_ref[...], sin_ref[...]).astype(BF16)

    rows = pl.ds(pl.multiple_of(qi * tq, tq), tq)
    qn = _rms(q_ref[...].astype(F32), qg_ref[...])
    qr = (_rope(qn, cos_ref[rows, :], sin_ref[rows, :]) * (HEAD_DIM ** -0.5)).astype(BF16)
    m = l = o = None
    for c in range(kp_ref.shape[0] // ATTN_TK):
        ks = pl.ds(c * ATTN_TK, ATTN_TK)
        s = lax.dot_general(qr, kp_ref[ks, :], (((1,), (1,)), ((), ())), preferred_element_type=F32)
        mc = jnp.max(s, axis=-1, keepdims=True)
        if c == 0:
            m = mc
            p = jnp.exp(s - m)
            l = jnp.sum(p, axis=-1, keepdims=True)
            o = jnp.dot(p.astype(BF16), v_ref[ks, :], preferred_element_type=F32)
        else:
            m_new = jnp.maximum(m, mc)
            alpha = jnp.exp(m - m_new)
            p = jnp.exp(s - m_new)
            l = alpha * l + jnp.sum(p, axis=-1, keepdims=True)
            o = alpha * o + jnp.dot(p.astype(BF16), v_ref[ks, :], preferred_element_type=F32)
            m = m_new
    o_ref[...] = (o / l).astype(BF16)


def _attention(proj, cos, sin_signed, q_g, k_g, batch, seq):
    tq = ATTN_TQ
    nq = seq // tq
    qc, kc, vc = PJ_Q // HEAD_DIM, PJ_K // HEAD_DIM, PJ_V // HEAD_DIM
    return pl.pallas_call(
        functools.partial(_attn_kernel, tq=tq),
        grid=(batch, N_Q_HEADS, nq),
        in_specs=[
            pl.BlockSpec((tq, HEAD_DIM), lambda b, h, i: (b * nq + i, qc + h)),
            pl.BlockSpec((seq, HEAD_DIM), lambda b, h, i: (b, kc + h // Q_PER_KV)),
            pl.BlockSpec((seq, HEAD_DIM), lambda b, h, i: (b, vc + h // Q_PER_KV)),
            pl.BlockSpec((seq, HEAD_DIM), lambda b, h, i: (0, 0)),
            pl.BlockSpec((seq, HEAD_DIM), lambda b, h, i: (0, 0)),
            pl.BlockSpec((1, HEAD_DIM), lambda b, h, i: (0, 0)),
            pl.BlockSpec((1, HEAD_DIM), lambda b, h, i: (0, 0)),
        ],
        out_specs=pl.BlockSpec((tq, HEAD_DIM), lambda b, h, i: (b * nq + i, h)),
        out_shape=jax.ShapeDtypeStruct((batch * seq, ATTN_WIDTH), BF16),
        scratch_shapes=[pltpu.VMEM((seq, HEAD_DIM), BF16)],
        compiler_params=_cparams(("arbitrary", "arbitrary", "arbitrary")),
        name="attn",
    )(proj, proj, proj, cos, sin_signed, q_g, k_g)


def _mix_kernel(attn_ref, u_ref, up_ref, un_ref, ga_ref, gp_ref, x_ref, pw_ref, ps_ref, wa_ref, wp_ref,
                wo_ref, o_ref, *, tm, seq):
    i = pl.program_id(0)
    t0 = (i * tm) % seq
    u = u_ref[...]
    prev = up_ref[...]
    nxt = un_ref[...]
    has_prev = t0 > 0
    has_next = t0 + tm < seq
    r = lax.broadcasted_iota(jnp.int32, (tm, tm), 0)
    c = lax.broadcasted_iota(jnp.int32, (tm, tm), 1)
    rh = lax.broadcasted_iota(jnp.int32, (tm, HALO), 0)
    ch = lax.broadcasted_iota(jnp.int32, (tm, HALO), 1)
    tpos = t0 + lax.broadcasted_iota(jnp.int32, (tm, 1), 0)
    ys = []
    for gi, w in enumerate(POOL_WINDOWS):
        sl = slice(gi * POOL_GROUP, (gi + 1) * POOL_GROUP)
        lo, hi = w // 2, w - w // 2
        band = jnp.where((c >= r - lo) & (c < r + hi), 1.0, 0.0).astype(BF16)
        band_p = jnp.where(has_prev & (ch - HALO >= rh - lo), 1.0, 0.0).astype(BF16)
        band_n = jnp.where(has_next & (ch + tm < rh + hi), 1.0, 0.0).astype(BF16)
        sums = (jnp.dot(band, u[:, sl], preferred_element_type=F32)
                + jnp.dot(band_p, prev[:, sl], preferred_element_type=F32)
                + jnp.dot(band_n, nxt[:, sl], preferred_element_type=F32))
        cnt = jnp.minimum(tpos + (w - w // 2), seq) - jnp.maximum(tpos - w // 2, 0)
        pooled = sums / cnt.astype(F32) - u[:, sl].astype(F32)
        ys.append(jnp.dot(pooled.astype(BF16), pw_ref[gi], preferred_element_type=F32))
    pool = (jnp.concatenate(ys, axis=1) * ps_ref[...]).astype(BF16)
    lift_a = jnp.dot(attn_ref[...], wa_ref[...], preferred_element_type=F32)
    lift_p = jnp.dot(pool, wp_ref[...], preferred_element_type=F32)
    mixed = ga_ref[...].astype(F32) * lift_a + gp_ref[...].astype(F32) * lift_p
    o_ref[...] = x_ref[...] + jnp.dot(mixed.astype(BF16), wo_ref[...], preferred_element_type=F32)


def _resident(shape):
    nd = len(shape)
    return pl.BlockSpec(shape, lambda *_: (0,) * nd, pipeline_mode=pl.Buffered(1))


def _mix(attn, proj, x2, pool_w, pool_scale, w_attn_up, w_pool_up, w_mix_out, seq):
    n = x2.shape[0]
    tm = MIX_TM
    hb = tm // HALO
    last_halo = n // HALO - 1
    return pl.pallas_call(
        functools.partial(_mix_kernel, tm=tm, seq=seq),
        grid=(n // tm,),
        in_specs=[
            pl.BlockSpec((tm, ATTN_WIDTH), lambda i: (i, 0)),
            pl.BlockSpec((tm, POOL_WIDTH), lambda i: (i, PJ_U // POOL_WIDTH)),
            pl.BlockSpec((HALO, POOL_WIDTH), lambda i: (jnp.maximum(i * hb - 1, 0), PJ_U // POOL_WIDTH)),
            pl.BlockSpec((HALO, POOL_WIDTH), lambda i: (jnp.minimum((i + 1) * hb, last_halo), PJ_U // POOL_WIDTH)),
            pl.BlockSpec((tm, D_MODEL), lambda i: (i, PJ_GA // D_MODEL)),
            pl.BlockSpec((tm, D_MODEL), lambda i: (i, PJ_GP // D_MODEL)),
            pl.BlockSpec((tm, D_MODEL), lambda i: (i, 0)),
            _resident(pool_w.shape),
            _resident(pool_scale.shape),
            _resident(w_attn_up.shape),
            _resident(w_pool_up.shape),
            _resident(w_mix_out.shape),
        ],
        out_specs=pl.BlockSpec((tm, D_MODEL), lambda i: (i, 0)),
        out_shape=jax.ShapeDtypeStruct((n, D_MODEL), F32),
        compiler_params=_cparams(("parallel",)),
        name="mix",
    )(attn, proj, proj, proj, proj, proj, x2, pool_w, pool_scale, w_attn_up, w_pool_up, w_mix_out)


def _memkv_kernel(m_ref, g_ref, wk_ref, wv_ref, k_ref, v_ref):
    m = _rms(m_ref[0], g_ref[...]).astype(BF16)
    k_ref[0] = jnp.dot(m, wk_ref[...], preferred_element_type=F32).astype(BF16)
    v_ref[0] = jnp.dot(m, wv_ref[...], preferred_element_type=F32).astype(BF16)


def _memkv(mem, g, w_ck, w_cv):
    b, m, d = mem.shape
    out = jax.ShapeDtypeStruct((b, m, MEM_WIDTH), BF16)
    return pl.pallas_call(
        _memkv_kernel,
        grid=(b,),
        in_specs=[
            pl.BlockSpec((1, m, d), lambda i: (i, 0, 0)),
            pl.BlockSpec((1, d), lambda i: (0, 0)),
            pl.BlockSpec((d, MEM_WIDTH), lambda i: (0, 0)),
            pl.BlockSpec((d, MEM_WIDTH), lambda i: (0, 0)),
        ],
        out_specs=[pl.BlockSpec((1, m, MEM_WIDTH), lambda i: (i, 0, 0))] * 2,
        out_shape=[out, out],
        compiler_params=_cparams(("parallel",)),
        name="memkv",
    )(mem, g, w_ck, w_cv)


def _split_bf16(a):
    hi = a.astype(BF16)
    lo = (a - hi.astype(F32)).astype(BF16)
    return hi, lo


def _cross_kernel(x_ref, cg_ref, wq_ref, k_ref, v_ref, wo_ref, mg_ref, wr_ref, br_ref,
                  x2_ref, hp_ref, lt_ref):
    x = x_ref[...]
    h = _rms(x, cg_ref[...]).astype(BF16)
    q = jnp.dot(h, wq_ref[...], preferred_element_type=F32)
    k = k_ref[0]
    v = v_ref[0]
    outs = []
    for hd in range(N_MEM_HEADS):
        sl = slice(hd * MEM_HEAD_DIM, (hd + 1) * MEM_HEAD_DIM)
        s = lax.dot_general(q[:, sl].astype(BF16), k[:, sl], (((1,), (1,)), ((), ())),
                            preferred_element_type=F32) * (MEM_HEAD_DIM ** -0.5)
        p = jnp.exp(s - jnp.max(s, axis=-1, keepdims=True))
        l = jnp.sum(p, axis=-1, keepdims=True)
        outs.append(jnp.dot(p.astype(BF16), v[:, sl], preferred_element_type=F32) / l)
    o = jnp.concatenate(outs, axis=1).astype(BF16)
    x2 = x + jnp.dot(o, wo_ref[...], preferred_element_type=F32)
    x2_ref[...] = x2

    h3 = _rms(x2, mg_ref[...])
    h_hi, h_lo = _split_bf16(h3)
    half = D_MODEL // 2
    bits = pltpu.bitcast(h_hi.astype(F32), jnp.uint32)
    hp_ref[...] = (bits[:, :half] >> 16) | (bits[:, half:] & jnp.uint32(0xFFFF0000))

    w_hi, w_lo = _split_bf16(wr_ref[...])
    logits = (jnp.dot(h_hi, w_hi, preferred_element_type=F32)
              + jnp.dot(h_hi, w_lo, preferred_element_type=F32)
              + jnp.dot(h_lo, w_hi, preferred_element_type=F32)) + br_ref[...]
    lt_ref[...] = logits.T[:N_EXPERTS, :]


def _cross(x1, cross_g, w_cq, kmem, vmem, w_co, moe_g, w_router_pad, b_router_pad, seq):
    n = x1.shape[0]
    tm = CROSS_TM
    per_seq = seq // tm
    mlen = kmem.shape[1]
    return pl.pallas_call(
        _cross_kernel,
        grid=(n // tm,),
        in_specs=[
            pl.BlockSpec((tm, D_MODEL), lambda i: (i, 0)),
            _resident((1, D_MODEL)),
            _resident(w_cq.shape),
            pl.BlockSpec((1, mlen, MEM_WIDTH), lambda i: (i // per_seq, 0, 0)),
            pl.BlockSpec((1, mlen, MEM_WIDTH), lambda i: (i // per_seq, 0, 0)),
            _resident(w_co.shape),
            _resident((1, D_MODEL)),
            _resident(w_router_pad.shape),
            _resident(b_router_pad.shape),
        ],
        out_specs=[
            pl.BlockSpec((tm, D_MODEL), lambda i: (i, 0)),
            pl.BlockSpec((tm, D_MODEL // 2), lambda i: (i, 0)),
            pl.BlockSpec((N_EXPERTS, tm), lambda i: (0, i)),
        ],
        out_shape=[
            jax.ShapeDtypeStruct((n, D_MODEL), F32),
            jax.ShapeDtypeStruct((n, D_MODEL // 2), jnp.uint32),
            jax.ShapeDtypeStruct((N_EXPERTS, n), F32),
        ],
        compiler_params=_cparams(("parallel",)),
        name="cross",
    )(x1, cross_g, w_cq, kmem, vmem, w_co, moe_g, w_router_pad, b_router_pad)


def _route_kernel(lt_ref, pos_ref, gate_ref, cnt_ref, nblk_ref, start_ref, m_ref, rank_ref):
    n = lt_ref.shape[1]
    l = lt_ref[...]
    row = lax.broadcasted_iota(jnp.int32, l.shape, 0)
    sels, vals = [], []
    for _ in range(TOP_K):
        m = jnp.max(l, axis=0, keepdims=True)
        idx = jnp.min(jnp.where(l == m, row, N_EXPERTS), axis=0, keepdims=True)
        sel = row == idx
        sels.append(sel)
        vals.append(m)
        l = jnp.where(sel, -jnp.inf, l)

    es = [jnp.exp(v - vals[0]) for v in vals]
    den = es[0] + es[1] + es[2] + es[3]
    for k in range(TOP_K):
        gate_ref[k:k + 1, :] = es[k] / den

    chosen = sels[0] | sels[1] | sels[2] | sels[3]
    m_ref[...] = jnp.where(chosen, 1.0, 0.0)

    ci = lax.broadcasted_iota(jnp.int32, (ROUTE_CHUNK, ROUTE_CHUNK), 0)
    cj = lax.broadcasted_iota(jnp.int32, (ROUTE_CHUNK, ROUTE_CHUNK), 1)
    upper = jnp.where(ci < cj, 1.0, 0.0).astype(BF16)

    def chunk(c, carry):
        cols = pl.ds(pl.multiple_of(c * ROUTE_CHUNK, ROUTE_CHUNK), ROUTE_CHUNK)
        mt = m_ref[:, cols]
        rank_ref[:, cols] = jnp.dot(mt.astype(BF16), upper, preferred_element_type=F32) + carry
        return carry + jnp.sum(mt, axis=1, keepdims=True)

    counts = lax.fori_loop(0, n // ROUTE_CHUNK, chunk, jnp.zeros((N_EXPERTS, 1), F32))
    nblk = jnp.floor((counts + (SLOT_BLK - 1)) * (1.0 / SLOT_BLK))
    ei = lax.broadcasted_iota(jnp.int32, (N_EXPERTS, N_EXPERTS), 0)
    ej = lax.broadcasted_iota(jnp.int32, (N_EXPERTS, N_EXPERTS), 1)
    lower = jnp.where(ej < ei, 1.0, 0.0).astype(BF16)
    nblk_b = jnp.broadcast_to(nblk, (N_EXPERTS, LANES))
    start_b = jnp.dot(lower, nblk_b.astype(BF16), preferred_element_type=F32)
    cnt_ref[...] = jnp.broadcast_to(counts, (N_EXPERTS, LANES)).astype(jnp.int32)
    nblk_ref[...] = nblk_b.astype(jnp.int32)
    start_ref[...] = start_b.astype(jnp.int32)

    slot = start_b[:, :1] * SLOT_BLK + rank_ref[...]
    for k in range(TOP_K):
        pos_ref[k:k + 1, :] = jnp.sum(jnp.where(sels[k], slot, 0.0), axis=0, keepdims=True).astype(jnp.int32)


def _route(logits_t):
    e, n = logits_t.shape
    small = jax.ShapeDtypeStruct((e, LANES), jnp.int32)
    return pl.pallas_call(
        _route_kernel,
        out_shape=[
            jax.ShapeDtypeStruct((TOP_K, n), jnp.int32),
            jax.ShapeDtypeStruct((TOP_K, n), F32),
            small, small, small,
        ],
        scratch_shapes=[pltpu.VMEM((e, n), F32), pltpu.VMEM((e, n), F32)],
        compiler_params=pltpu.CompilerParams(vmem_limit_bytes=VMEM_LIMIT),
        name="route",
    )(logits_t)


def _dispatch_kernel(pos_ref, cnt_ref, nblk_ref, start_ref, h_ref, xs_ref, zero_ref, sem, pad_sem, *, tm):
    i = pl.program_id(0)

    def tok(r, _):
        t = i * tm + r
        for k in range(TOP_K):
            p = pos_ref[t * TOP_K + k]
            pltpu.make_async_copy(h_ref.at[pl.ds(r, 1)], xs_ref.at[pl.ds(p, 1)], sem).start()
        return 0

    lax.fori_loop(0, tm, tok, 0)
    for _ in range(TOP_K):
        pltpu.make_async_copy(h_ref, xs_ref.at[pl.ds(0, tm)], sem).wait()

    @pl.when(i == pl.num_programs(0) - 1)
    def _():
        _dispatch_fill(cnt_ref, nblk_ref, start_ref, xs_ref, zero_ref, pad_sem)


def _dispatch_fill(cnt_ref, nblk_ref, start_ref, xs_ref, zero_ref, pad_sem):
    zero_ref[...] = jnp.zeros(zero_ref.shape, zero_ref.dtype)

    def pad_expert(e, _):
        lo = start_ref[e] * SLOT_BLK + cnt_ref[e]
        hi = (start_ref[e] + nblk_ref[e]) * SLOT_BLK

        def fill(r, _):
            pltpu.make_async_copy(zero_ref.at[pl.ds(0, 1)], xs_ref.at[pl.ds(r, 1)], pad_sem).start()
            return 0

        def drain(r, _):
            pltpu.make_async_copy(zero_ref.at[pl.ds(0, 1)], xs_ref.at[pl.ds(r, 1)], pad_sem).wait()
            return 0

        lax.fori_loop(lo, hi, fill, 0)
        lax.fori_loop(lo, hi, drain, 0)
        return 0

    lax.fori_loop(0, N_EXPERTS, pad_expert, 0)

    def tail(b, _):
        rows = pl.ds(pl.multiple_of(b * SLOT_BLK, SLOT_BLK), SLOT_BLK)
        cp = pltpu.make_async_copy(zero_ref, xs_ref.at[rows], pad_sem)
        cp.start()
        cp.wait()
        return 0

    lax.fori_loop(start_ref[N_EXPERTS - 1] + nblk_ref[N_EXPERTS - 1], xs_ref.shape[0] // SLOT_BLK, tail, 0)


def _dispatch(pos_flat, counts, nblk, start, h_packed, n_slots):
    n_tok, width = h_packed.shape
    tm = DISPATCH_CHUNK
    return pl.pallas_call(
        functools.partial(_dispatch_kernel, tm=tm),
        grid_spec=pltpu.PrefetchScalarGridSpec(
            num_scalar_prefetch=4,
            grid=(n_tok // tm,),
            in_specs=[pl.BlockSpec((tm, width), lambda i, *_: (i, 0))],
            out_specs=pl.BlockSpec(memory_space=pl.ANY),
            scratch_shapes=[
                pltpu.VMEM((SLOT_BLK, width), h_packed.dtype),
                pltpu.SemaphoreType.DMA(()),
                pltpu.SemaphoreType.DMA(()),
            ],
        ),
        out_shape=jax.ShapeDtypeStruct((n_slots, width), h_packed.dtype),
        compiler_params=pltpu.CompilerParams(dimension_semantics=("arbitrary",)),
        name="dispatch",
    )(pos_flat, counts, nblk, start, h_packed)


def _swiglu_pairs(h_a, h_b):
    lane = lax.broadcasted_iota(jnp.int32, h_a.shape, 1)
    even = (lane % 2) == 0
    gate_in = jnp.where(even, h_a, pltpu.roll(h_b, 1, 1))
    lin_in = jnp.where(even, pltpu.roll(h_a, LANES - 1, 1), h_b)
    glu = jnp.minimum(gate_in, SWIGLU_LIMIT)
    lin = jnp.clip(lin_in, -SWIGLU_LIMIT, SWIGLU_LIMIT)
    return glu * (1.0 / (1.0 + jnp.exp(-SWIGLU_ALPHA * glu))) * (lin + 1.0)


def _experts_kernel(ie_ref, ib_ref, ins_ref, ni_ref, xs_ref, w1_hbm, b1_ref, w2_hbm, b2_ref, ys_ref,
                    xraw_ref, xb_ref, w1buf_ref, w2buf_ref, w2q_ref, acc_ref, sem, xsem, wsem):
    w = pl.program_id(0)
    j = pl.program_id(1)
    n_chunks = pl.num_programs(1)
    last_j = n_chunks - 1
    n_items = ni_ref[0]
    nsub = ins_ref[w]
    slot = w % 2
    half = D_MODEL // 2
    quarter = FFN_CHUNK // 4
    g = w * n_chunks + j
    wslot = g % WEIGHT_RING

    def sub_rows(s):
        return pl.ds(s * SLOT_BLK, SLOT_BLK)

    def slot_rows(item, s):
        return pl.ds(pl.multiple_of((ib_ref[item] + s) * SLOT_BLK, SLOT_BLK), SLOT_BLK)

    def writeback(item, item_slot, s):
        return pltpu.make_async_copy(acc_ref.at[item_slot, sub_rows(s)], ys_ref.at[slot_rows(item, s)],
                                     sem.at[item_slot])

    def wait_item(item, item_slot):
        for s in range(SUB_PER_ITEM):
            @pl.when(s < ins_ref[item])
            def _(s=s):
                writeback(item, item_slot, s).wait()

    def x_copy(item, s):
        return pltpu.make_async_copy(xs_ref.at[slot_rows(item, s)], xraw_ref.at[sub_rows(s)], xsem)

    def x_copies(item, fn):
        for s in range(SUB_PER_ITEM):
            @pl.when(s < ins_ref[item])
            def _(s=s):
                fn(x_copy(item, s))

    def weight_copies(chunk, ring_slot):
        it = chunk // n_chunks
        jj = chunk % n_chunks
        e = ie_ref[it]
        cols = pl.ds(pl.multiple_of(jj * (2 * FFN_CHUNK), 2 * FFN_CHUNK), 2 * FFN_CHUNK)
        rows = pl.ds(pl.multiple_of(jj * FFN_CHUNK, FFN_CHUNK), FFN_CHUNK)
        return (pltpu.make_async_copy(w1_hbm.at[e, :, cols], w1buf_ref.at[ring_slot], wsem.at[ring_slot]),
                pltpu.make_async_copy(w2_hbm.at[e, rows, :], w2buf_ref.at[ring_slot], wsem.at[ring_slot]))

    def ffn_chunk(n):
        rows = n * SLOT_BLK
        w1 = w1buf_ref[wslot].astype(BF16)
        for dst, src in ((0, 0), (1, 2 * quarter), (2 * quarter, quarter), (2 * quarter + 1, 3 * quarter)):
            for c in range(D_MODEL // LANES):
                w2q_ref[c, pl.ds(dst, quarter, stride=2), :] = (
                    w2buf_ref[wslot, pl.ds(src, quarter), c * LANES:(c + 1) * LANES])
        w2 = jnp.concatenate([w2q_ref[c] for c in range(D_MODEL // LANES)], axis=1).astype(BF16)
        h = jnp.dot(xb_ref[0:rows, :], w1, preferred_element_type=F32) + b1_ref[...]
        act = jnp.concatenate(
            [_swiglu_pairs(h[:, 0:LANES], h[:, 2 * LANES:3 * LANES]),
             _swiglu_pairs(h[:, LANES:2 * LANES], h[:, 3 * LANES:4 * LANES])], axis=1)
        acc_ref[slot, 0:rows, :] += jnp.dot(act.astype(BF16), w2, preferred_element_type=F32)

    @pl.when(w < n_items)
    def _():
        @pl.when(g == 0)
        def _():
            x_copies(0, lambda cp: cp.start())
            for k in range(WEIGHT_RING - 1):
                for cp in weight_copies(k, k):
                    cp.start()

        ahead = g + (WEIGHT_RING - 1)

        @pl.when(ahead < n_items * n_chunks)
        def _():
            for cp in weight_copies(ahead, ahead % WEIGHT_RING):
                cp.start()

        @pl.when(j == 0)
        def _():
            @pl.when(w >= 2)
            def _():
                wait_item(w - 2, slot)

            x_copies(w, lambda cp: cp.wait())
            for s in range(SUB_PER_ITEM):
                @pl.when(s < nsub)
                def _(s=s):
                    word = xraw_ref[sub_rows(s), :]
                    xb_ref[sub_rows(s), :half] = pltpu.bitcast(word << 16, F32).astype(BF16)
                    xb_ref[sub_rows(s), half:] = pltpu.bitcast(word & jnp.uint32(0xFFFF0000), F32).astype(BF16)
                    acc_ref[slot, sub_rows(s), :] = jnp.broadcast_to(b2_ref[...], (SLOT_BLK, D_MODEL))

        @pl.when((j == 1) & (w + 1 < n_items))
        def _():
            x_copies(w + 1, lambda cp: cp.start())

        for cp in weight_copies(g, wslot):
            cp.wait()

        for n in range(1, SUB_PER_ITEM + 1):
            @pl.when(nsub == n)
            def _(n=n):
                ffn_chunk(n)

        @pl.when(j == last_j)
        def _():
            for s in range(SUB_PER_ITEM):
                @pl.when(s < nsub)
                def _(s=s):
                    writeback(w, slot, s).start()

        @pl.when((j == last_j) & (w == n_items - 1))
        def _():
            wait_item(w, slot)

            @pl.when(w >= 1)
            def _():
                wait_item(w - 1, 1 - slot)

            zero_blk = acc_ref.at[slot, sub_rows(0)]
            zero_blk[...] = jnp.zeros(zero_blk.shape, F32)

            def tail(b, _):
                rows = pl.ds(pl.multiple_of(b * SLOT_BLK, SLOT_BLK), SLOT_BLK)
                cp = pltpu.make_async_copy(zero_blk, ys_ref.at[rows], sem.at[slot])
                cp.start()
                cp.wait()
                return 0

            lax.fori_loop(ib_ref[w] + nsub, ys_ref.shape[0] // SLOT_BLK, tail, 0)


def _experts(item_e, item_b, item_n, n_items, xs, w_gate_up, b_gate_up, w_down, b_down, n_blocks, max_items):
    width = xs.shape[1]
    n_chunks = D_EXPERT // FFN_CHUNK
    item_rows = SUB_PER_ITEM * SLOT_BLK
    in_specs = [
        pl.BlockSpec(memory_space=pl.ANY),
        pl.BlockSpec(memory_space=pl.ANY),
        pl.BlockSpec((None, 1, 2 * FFN_CHUNK), lambda w, j, ie, ib, ins, ni: (ie[w], 0, j)),
        pl.BlockSpec(memory_space=pl.ANY),
        pl.BlockSpec((None, 1, D_MODEL), lambda w, j, ie, ib, ins, ni: (ie[w], 0, 0)),
    ]
    return pl.pallas_call(
        _experts_kernel,
        grid_spec=pltpu.PrefetchScalarGridSpec(
            num_scalar_prefetch=4,
            grid=(max_items, n_chunks),
            in_specs=in_specs,
            out_specs=pl.BlockSpec(memory_space=pl.ANY),
            scratch_shapes=[
                pltpu.VMEM((item_rows, width), xs.dtype),
                pltpu.VMEM((item_rows, D_MODEL), BF16),
                pltpu.VMEM((WEIGHT_RING, D_MODEL, 2 * FFN_CHUNK), F32),
                pltpu.VMEM((WEIGHT_RING, FFN_CHUNK, D_MODEL), F32),
                pltpu.VMEM((D_MODEL // LANES, FFN_CHUNK, LANES), F32),
                pltpu.VMEM((2, item_rows, D_MODEL), F32),
                pltpu.SemaphoreType.DMA((2,)),
                pltpu.SemaphoreType.DMA(()),
                pltpu.SemaphoreType.DMA((WEIGHT_RING,)),
            ],
        ),
        out_shape=jax.ShapeDtypeStruct((n_blocks * SLOT_BLK, D_MODEL), F32),
        compiler_params=_cparams(("arbitrary", "arbitrary"), EXPERTS_VMEM_LIMIT),
        name="experts",
    )(item_e, item_b, item_n, n_items, xs, w_gate_up, b_gate_up, w_down, b_down)


def _combine_kernel(pos_ref, ys_ref, gate_ref, x_ref, g_ref, o_ref, buf_ref, sem, *, tm):
    i = pl.program_id(0)
    n_steps = pl.num_programs(0)

    def issue(step, slot):
        def tok(r, _):
            t = step * tm + r
            for k in range(TOP_K):
                p = pos_ref[t * TOP_K + k]
                pltpu.make_async_copy(ys_ref.at[pl.ds(p, 1)], buf_ref.at[slot, k, pl.ds(r, 1)], sem.at[slot]).start()
            return 0
        lax.fori_loop(0, tm, tok, 0)

    @pl.when(i == 0)
    def _():
        issue(0, 0)

    @pl.when(i + 1 < n_steps)
    def _():
        issue(i + 1, (i + 1) % 2)

    slot = i % 2
    for k in range(TOP_K):
        pltpu.make_async_copy(ys_ref.at[pl.ds(0, tm)], buf_ref.at[slot, k], sem.at[slot]).wait()
    gates = gate_ref[...]
    acc = x_ref[...]
    for k in range(TOP_K):
        acc = acc + gates[:, k:k + 1] * buf_ref[slot, k]
    o_ref[...] = _rms(acc, g_ref[...])


def _combine(pos_flat, ys, gates, x2, final_g):
    n = x2.shape[0]
    tm = COMBINE_TM
    return pl.pallas_call(
        functools.partial(_combine_kernel, tm=tm),
        grid_spec=pltpu.PrefetchScalarGridSpec(
            num_scalar_prefetch=1,
            grid=(n // tm,),
            in_specs=[
                pl.BlockSpec(memory_space=pl.ANY),
                pl.BlockSpec((tm, TOP_K), lambda i, pos: (i, 0)),
                pl.BlockSpec((tm, D_MODEL), lambda i, pos: (i, 0)),
                pl.BlockSpec((1, D_MODEL), lambda i, pos: (0, 0)),
            ],
            out_specs=pl.BlockSpec((tm, D_MODEL), lambda i, pos: (i, 0)),
            scratch_shapes=[
                pltpu.VMEM((2, TOP_K, tm, D_MODEL), F32),
                pltpu.SemaphoreType.DMA((2,)),
            ],
        ),
        out_shape=jax.ShapeDtypeStruct((n, D_MODEL), F32),
        compiler_params=_cparams(("arbitrary",)),
        name="combine",
    )(pos_flat, ys, gates, x2, final_g)


def _rope_tables(seq):
    t = jnp.arange(seq, dtype=jnp.int32)
    rows = (t // GRID_W).astype(F32)
    cols = (t % GRID_W).astype(F32)
    axis_dim = HEAD_DIM // 2
    inv_freq = ROPE_THETA ** (-jnp.arange(0, axis_dim, 2, dtype=F32) / axis_dim)
    ar = rows[:, None] * inv_freq
    ac = cols[:, None] * inv_freq
    cos = jnp.concatenate([jnp.cos(ar), jnp.cos(ar), jnp.cos(ac), jnp.cos(ac)], axis=1)
    sin = jnp.concatenate([-jnp.sin(ar), jnp.sin(ar), -jnp.sin(ac), jnp.sin(ac)], axis=1)
    return cos, sin


def _work_items(nblk, start, max_items):
    per_e = (nblk + SUB_PER_ITEM - 1) // SUB_PER_ITEM
    ends = jnp.cumsum(per_e)
    n_items = ends[-1]
    w = jnp.minimum(jnp.arange(max_items, dtype=jnp.int32), n_items - 1)
    e = jnp.minimum(jnp.searchsorted(ends, w, side="right"), N_EXPERTS - 1).astype(jnp.int32)
    local = w - (ends - per_e)[e]
    blk0 = start[e] + local * SUB_PER_ITEM
    nsub = jnp.clip(nblk[e] - local * SUB_PER_ITEM, 1, SUB_PER_ITEM)
    return e, blk0.astype(jnp.int32), nsub.astype(jnp.int32), n_items.reshape(1).astype(jnp.int32)


def _layer(x2, mem, p, batch, seq, cos, sin_signed):
    n = x2.shape[0]
    w_in = p["w_in"]
    w_in_r = jnp.concatenate(
        [w_in[:, P_END:GA_END], w_in[:, GA_END:], w_in[:, V_END:P_END], w_in[:, :V_END]], axis=1).astype(BF16)
    proj = _proj(x2, p["mix_norm_g"][None, :], w_in_r)
    attn = _attention(proj, cos, sin_signed, p["q_norm_g"][None, :], p["k_norm_g"][None, :], batch, seq)
    x1 = _mix(attn, proj, x2, p["pool_w"].astype(BF16), p["pool_scale"][None, :],
              p["w_attn_up"].astype(BF16), p["w_pool_up"].astype(BF16), p["w_mix_out"].astype(BF16), seq)

    kmem, vmem = _memkv(mem, p["mem_norm_g"][None, :], p["w_ck"].astype(BF16), p["w_cv"].astype(BF16))
    w_router_pad = jnp.pad(p["w_router"], ((0, 0), (0, LANES - N_EXPERTS)))
    b_router_pad = jnp.pad(p["b_router"], (0, LANES - N_EXPERTS))[None, :]
    x2b, h_packed, logits_t = _cross(x1, p["cross_norm_g"][None, :], p["w_cq"].astype(BF16), kmem, vmem,
                                     p["w_co"].astype(BF16), p["moe_norm_g"][None, :], w_router_pad,
                                     b_router_pad, seq)

    pos, gates, counts, nblk, start = _route(logits_t)
    counts, nblk, start = counts[:, 0], nblk[:, 0], start[:, 0]
    pos_flat = pos.T.reshape(-1)
    n_blocks = (n * TOP_K) // SLOT_BLK + N_EXPERTS
    max_items = (n * TOP_K) // (SLOT_BLK * SUB_PER_ITEM) + N_EXPERTS
    xs = _dispatch(pos_flat, counts, nblk, start, h_packed, n_blocks * SLOT_BLK)
    item_e, item_b, item_n, n_items = _work_items(nblk, start, max_items)
    ys = _experts(item_e, item_b, item_n, n_items, xs, p["w_gate_up"], p["b_gate_up"][:, None, :],
                  p["w_down"], p["b_down"][:, None, :], n_blocks, max_items)
    return pos_flat, ys, gates.T, x2b


def kernel(x, mem, mix_norm_g, w_in, q_norm_g, k_norm_g, pool_w, pool_scale, w_attn_up, w_pool_up, w_mix_out,
           cross_norm_g, mem_norm_g, w_cq, w_ck, w_cv, w_co, moe_norm_g, w_router, b_router, w_gate_up,
           b_gate_up, w_down, b_down, final_norm_g):
    batch, seq, d = x.shape
    depth = w_in.shape[0]
    assert depth == 1, "the combine stage applies the final norm, so exactly one layer is supported"
    cos, sin_signed = _rope_tables(seq)
    x2 = x.reshape(batch * seq, d)
    p = dict(mix_norm_g=mix_norm_g[0], w_in=w_in[0], q_norm_g=q_norm_g[0], k_norm_g=k_norm_g[0],
             pool_w=pool_w[0], pool_scale=pool_scale[0], w_attn_up=w_attn_up[0], w_pool_up=w_pool_up[0],
             w_mix_out=w_mix_out[0], cross_norm_g=cross_norm_g[0], mem_norm_g=mem_norm_g[0], w_cq=w_cq[0],
             w_ck=w_ck[0], w_cv=w_cv[0], w_co=w_co[0], moe_norm_g=moe_norm_g[0], w_router=w_router[0],
             b_router=b_router[0], w_gate_up=w_gate_up[0], b_gate_up=b_gate_up[0], w_down=w_down[0],
             b_down=b_down[0])
    pos_flat, ys, gates, x2b = _layer(x2, mem, p, batch, seq, cos, sin_signed)
    out = _combine(pos_flat, ys, gates, x2b, final_norm_g[None, :])
    return out.reshape(batch, seq, d)
```

```python
import functools

import jax
import jax.numpy as jnp
from jax import lax
from jax.experimental import pallas as pl
from jax.experimental.pallas import tpu as pltpu

F32 = jnp.float32
BF16 = jnp.bfloat16

D_MODEL = 2048
GRID_W = 64
HEAD_DIM = 128
N_Q_HEADS = 8
N_KV_HEADS = 2
Q_PER_KV = N_Q_HEADS // N_KV_HEADS
ATTN_WIDTH = N_Q_HEADS * HEAD_DIM
KV_WIDTH = N_KV_HEADS * HEAD_DIM
ROPE_THETA = 10000.0
POOL_WINDOWS = (2, 4, 8, 16)
N_POOL_GROUPS = len(POOL_WINDOWS)
POOL_WIDTH = D_MODEL // 2
POOL_GROUP = POOL_WIDTH // N_POOL_GROUPS
Q_END = ATTN_WIDTH
K_END = Q_END + KV_WIDTH
V_END = K_END + KV_WIDTH
P_END = V_END + POOL_WIDTH
GA_END = P_END + D_MODEL
IN_WIDTH = GA_END + D_MODEL
N_MEM_HEADS = 4
MEM_HEAD_DIM = 128
MEM_WIDTH = N_MEM_HEADS * MEM_HEAD_DIM
N_EXPERTS = 32
TOP_K = 4
D_EXPERT = D_MODEL
SWIGLU_LIMIT = 7.0
SWIGLU_ALPHA = 1.702
NORM_EPS = 1e-6

LANES = 128
BF16_ROWS = 16
VMEM_LIMIT = 56 * 1024 * 1024

PJ_GA = 0
PJ_GP = D_MODEL
PJ_U = 2 * D_MODEL
PJ_Q = PJ_U + POOL_WIDTH
PJ_K = PJ_Q + ATTN_WIDTH
PJ_V = PJ_K + KV_WIDTH

PROJ_TM = 1024
PROJ_TN = 512
ATTN_TQ = 512
ATTN_TK = 1024
MIX_TM = 256
HALO = BF16_ROWS
CROSS_TM = 256
ROUTE_CHUNK = 256
SLOT_BLK = 256
SUB_PER_ITEM = 5
FFN_CHUNK = 256
WEIGHT_RING = 3
EXPERTS_VMEM_LIMIT = 62 * 1024 * 1024
DISPATCH_CHUNK = 512
COMBINE_TM = 256


def _cparams(sem, vmem=VMEM_LIMIT):
    return pltpu.CompilerParams(dimension_semantics=sem, vmem_limit_bytes=vmem)


def _rms(x, g):
    return x * lax.rsqrt(jnp.mean(x * x, axis=-1, keepdims=True) + NORM_EPS) * g


def _proj_kernel(x_ref, g_ref, w_ref, o_ref, h_ref, *, n_gate_blocks):
    j = pl.program_id(1)

    @pl.when(j == 0)
    def _():
        h_ref[...] = _rms(x_ref[...], g_ref[...]).astype(BF16)

    acc = jnp.dot(h_ref[...], w_ref[...], preferred_element_type=F32)
    o_ref[...] = jnp.where(j < n_gate_blocks, 1.0 / (1.0 + jnp.exp(-acc)), acc).astype(BF16)


def _proj(x2, g, w):
    n, d = x2.shape
    width = w.shape[1]
    return pl.pallas_call(
        functools.partial(_proj_kernel, n_gate_blocks=PJ_U // PROJ_TN),
        grid=(n // PROJ_TM, width // PROJ_TN),
        in_specs=[
            pl.BlockSpec((PROJ_TM, d), lambda i, j: (i, 0)),
            pl.BlockSpec((1, d), lambda i, j: (0, 0)),
            pl.BlockSpec((d, PROJ_TN), lambda i, j: (0, j)),
        ],
        out_specs=pl.BlockSpec((PROJ_TM, PROJ_TN), lambda i, j: (i, j)),
        out_shape=jax.ShapeDtypeStruct((n, width), BF16),
        scratch_shapes=[pltpu.VMEM((PROJ_TM, d), BF16)],
        compiler_params=_cparams(("parallel", "arbitrary")),
        name="proj",
    )(x2, g, w)


def _rope(x, cos, sin_signed):
    lane = lax.broadcasted_iota(jnp.int32, x.shape, 1)
    low = (lane % (HEAD_DIM // 2)) < (HEAD_DIM // 4)
    partner = jnp.where(low, pltpu.roll(x, HEAD_DIM - HEAD_DIM // 4, 1), pltpu.roll(x, HEAD_DIM // 4, 1))
    return x * cos + partner * sin_signed


def _attn_kernel(q_ref, k_ref, v_ref, cos_ref, sin_ref, qg_ref, kg_ref, o_ref, kp_ref, *, tq):
    h = pl.program_id(1)
    qi = pl.program_id(2)

    @pl.when((qi == 0) & (h % Q_PER_KV == 0))
    def _():
        kn = _rms(k_ref[...].astype(F32), kg_ref[...])
        kp_ref[...] = _rope(kn, cos_ref[...], sin_ref[...]).astype(BF16)

    rows = pl.ds(pl.multiple_of(qi * tq, tq), tq)
    qn = _rms(q_ref[...].astype(F32), qg_ref[...])
    qr = (_rope(qn, cos_ref[rows, :], sin_ref[rows, :]) * (HEAD_DIM ** -0.5)).astype(BF16)
    m = l = o = None
    for c in range(kp_ref.shape[0] // ATTN_TK):
        ks = pl.ds(c * ATTN_TK, ATTN_TK)
        s = lax.dot_general(qr, kp_ref[ks, :], (((1,), (1,)), ((), ())), preferred_element_type=F32)
        mc = jnp.max(s, axis=-1, keepdims=True)
        if c == 0:
            m = mc
            p = jnp.exp(s - m)
            l = jnp.sum(p, axis=-1, keepdims=True)
            o = jnp.dot(p.astype(BF16), v_ref[ks, :], preferred_element_type=F32)
        else:
            m_new = jnp.maximum(m, mc)
            alpha = jnp.exp(m - m_new)
            p = jnp.exp(s - m_new)
            l = alpha * l + jnp.sum(p, axis=-1, keepdims=True)
            o = alpha * o + jnp.dot(p.astype(BF16), v_ref[ks, :], preferred_element_type=F32)
            m = m_new
    o_ref[...] = (o / l).astype(BF16)


def _attention(proj, cos, sin_signed, q_g, k_g, batch, seq):
    tq = ATTN_TQ
    nq = seq // tq
    qc, kc, vc = PJ_Q // HEAD_DIM, PJ_K // HEAD_DIM, PJ_V // HEAD_DIM
    return pl.pallas_call(
        functools.partial(_attn_kernel, tq=tq),
        grid=(batch, N_Q_HEADS, nq),
        in_specs=[
            pl.BlockSpec((tq, HEAD_DIM), lambda b, h, i: (b * nq + i, qc + h)),
            pl.BlockSpec((seq, HEAD_DIM), lambda b, h, i: (b, kc + h // Q_PER_KV)),
            pl.BlockSpec((seq, HEAD_DIM), lambda b, h, i: (b, vc + h // Q_PER_KV)),
            pl.BlockSpec((seq, HEAD_DIM), lambda b, h, i: (0, 0)),
            pl.BlockSpec((seq, HEAD_DIM), lambda b, h, i: (0, 0)),
            pl.BlockSpec((1, HEAD_DIM), lambda b, h, i: (0, 0)),
            pl.BlockSpec((1, HEAD_DIM), lambda b, h, i: (0, 0)),
        ],
        out_specs=pl.BlockSpec((tq, HEAD_DIM), lambda b, h, i: (b * nq + i, h)),
        out_shape=jax.ShapeDtypeStruct((batch * seq, ATTN_WIDTH), BF16),
        scratch_shapes=[pltpu.VMEM((seq, HEAD_DIM), BF16)],
        compiler_params=_cparams(("arbitrary", "arbitrary", "arbitrary")),
        name="attn",
    )(proj, proj, proj, cos, sin_signed, q_g, k_g)


def _mix_kernel(attn_ref, u_ref, up_ref, un_ref, ga_ref, gp_ref, x_ref, pw_ref, ps_ref, wa_ref, wp_ref,
                wo_ref, o_ref, *, tm, seq):
    i = pl.program_id(0)
    t0 = (i * tm) % seq
    u = u_ref[...]
    prev = up_ref[...]
    nxt = un_ref[...]
    has_prev = t0 > 0
    has_next = t0 + tm < seq
    r = lax.broadcasted_iota(jnp.int32, (tm, tm), 0)
    c = lax.broadcasted_iota(jnp.int32, (tm, tm), 1)
    rh = lax.broadcasted_iota(jnp.int32, (tm, HALO), 0)
    ch = lax.broadcasted_iota(jnp.int32, (tm, HALO), 1)
    tpos = t0 + lax.broadcasted_iota(jnp.int32, (tm, 1), 0)
    ys = []
    for gi, w in enumerate(POOL_WINDOWS):
        sl = slice(gi * POOL_GROUP, (gi + 1) * POOL_GROUP)
        lo, hi = w // 2, w - w // 2
        band = jnp.where((c >= r - lo) & (c < r + hi), 1.0, 0.0).astype(BF16)
        band_p = jnp.where(has_prev & (ch - HALO >= rh - lo), 1.0, 0.0).astype(BF16)
        band_n = jnp.where(has_next & (ch + tm < rh + hi), 1.0, 0.0).astype(BF16)
        sums = (jnp.dot(band, u[:, sl], preferred_element_type=F32)
                + jnp.dot(band_p, prev[:, sl], preferred_element_type=F32)
                + jnp.dot(band_n, nxt[:, sl], preferred_element_type=F32))
        cnt = jnp.minimum(tpos + (w - w // 2), seq) - jnp.maximum(tpos - w // 2, 0)
        pooled = sums / cnt.astype(F32) - u[:, sl].astype(F32)
        ys.append(jnp.dot(pooled.astype(BF16), pw_ref[gi], preferred_element_type=F32))
    pool = (jnp.concatenate(ys, axis=1) * ps_ref[...]).astype(BF16)
    lift_a = jnp.dot(attn_ref[...], wa_ref[...], preferred_element_type=F32)
    lift_p = jnp.dot(pool, wp_ref[...], preferred_element_type=F32)
    mixed = ga_ref[...].astype(F32) * lift_a + gp_ref[...].astype(F32) * lift_p
    o_ref[...] = x_ref[...] + jnp.dot(mixed.astype(BF16), wo_ref[...], preferred_element_type=F32)


def _resident(shape):
    nd = len(shape)
    return pl.BlockSpec(shape, lambda *_: (0,) * nd, pipeline_mode=pl.Buffered(1))


def _mix(attn, proj, x2, pool_w, pool_scale, w_attn_up, w_pool_up, w_mix_out, seq):
    n = x2.shape[0]
    tm = MIX_TM
    hb = tm // HALO
    last_halo = n // HALO - 1
    return pl.pallas_call(
        functools.partial(_mix_kernel, tm=tm, seq=seq),
        grid=(n // tm,),
        in_specs=[
            pl.BlockSpec((tm, ATTN_WIDTH), lambda i: (i, 0)),
            pl.BlockSpec((tm, POOL_WIDTH), lambda i: (i, PJ_U // POOL_WIDTH)),
            pl.BlockSpec((HALO, POOL_WIDTH), lambda i: (jnp.maximum(i * hb - 1, 0), PJ_U // POOL_WIDTH)),
            pl.BlockSpec((HALO, POOL_WIDTH), lambda i: (jnp.minimum((i + 1) * hb, last_halo), PJ_U // POOL_WIDTH)),
            pl.BlockSpec((tm, D_MODEL), lambda i: (i, PJ_GA // D_MODEL)),
            pl.BlockSpec((tm, D_MODEL), lambda i: (i, PJ_GP // D_MODEL)),
            pl.BlockSpec((tm, D_MODEL), lambda i: (i, 0)),
            _resident(pool_w.shape),
            _resident(pool_scale.shape),
            _resident(w_attn_up.shape),
            _resident(w_pool_up.shape),
            _resident(w_mix_out.shape),
        ],
        out_specs=pl.BlockSpec((tm, D_MODEL), lambda i: (i, 0)),
        out_shape=jax.ShapeDtypeStruct((n, D_MODEL), F32),
        compiler_params=_cparams(("parallel",)),
        name="mix",
    )(attn, proj, proj, proj, proj, proj, x2, pool_w, pool_scale, w_attn_up, w_pool_up, w_mix_out)


def _memkv_kernel(m_ref, g_ref, wk_ref, wv_ref, k_ref, v_ref):
    m = _rms(m_ref[0], g_ref[...]).astype(BF16)
    k_ref[0] = jnp.dot(m, wk_ref[...], preferred_element_type=F32).astype(BF16)
    v_ref[0] = jnp.dot(m, wv_ref[...], preferred_element_type=F32).astype(BF16)


def _memkv(mem, g, w_ck, w_cv):
    b, m, d = mem.shape
    out = jax.ShapeDtypeStruct((b, m, MEM_WIDTH), BF16)
    return pl.pallas_call(
        _memkv_kernel,
        grid=(b,),
        in_specs=[
            pl.BlockSpec((1, m, d), lambda i: (i, 0, 0)),
            pl.BlockSpec((1, d), lambda i: (0, 0)),
            pl.BlockSpec((d, MEM_WIDTH), lambda i: (0, 0)),
            pl.BlockSpec((d, MEM_WIDTH), lambda i: (0, 0)),
        ],
        out_specs=[pl.BlockSpec((1, m, MEM_WIDTH), lambda i: (i, 0, 0))] * 2,
        out_shape=[out, out],
        compiler_params=_cparams(("parallel",)),
        name="memkv",
    )(mem, g, w_ck, w_cv)


def _split_bf16(a):
    hi = a.astype(BF16)
    lo = (a - hi.astype(F32)).astype(BF16)
    return hi, lo


def _cross_kernel(x_ref, cg_ref, wq_ref, k_ref, v_ref, wo_ref, mg_ref, wr_ref, br_ref,
                  x2_ref, hp_ref, lt_ref):
    x = x_ref[...]
    h = _rms(x, cg_ref[...]).astype(BF16)
    q = jnp.dot(h, wq_ref[...], preferred_element_type=F32)
    k = k_ref[0]
    v = v_ref[0]
    outs = []
    for hd in range(N_MEM_HEADS):
        sl = slice(hd * MEM_HEAD_DIM, (hd + 1) * MEM_HEAD_DIM)
        s = lax.dot_general(q[:, sl].astype(BF16), k[:, sl], (((1,), (1,)), ((), ())),
                            preferred_element_type=F32) * (MEM_HEAD_DIM ** -0.5)
        p = jnp.exp(s - jnp.max(s, axis=-1, keepdims=True))
        l = jnp.sum(p, axis=-1, keepdims=True)
        outs.append(jnp.dot(p.astype(BF16), v[:, sl], preferred_element_type=F32) / l)
    o = jnp.concatenate(outs, axis=1).astype(BF16)
    x2 = x + jnp.dot(o, wo_ref[...], preferred_element_type=F32)
    x2_ref[...] = x2

    h3 = _rms(x2, mg_ref[...])
    h_hi, h_lo = _split_bf16(h3)
    half = D_MODEL // 2
    bits = pltpu.bitcast(h_hi.astype(F32), jnp.uint32)
    hp_ref[...] = (bits[:, :half] >> 16) | (bits[:, half:] & jnp.uint32(0xFFFF0000))

    w_hi, w_lo = _split_bf16(wr_ref[...])
    logits = (jnp.dot(h_hi, w_hi, preferred_element_type=F32)
              + jnp.dot(h_hi, w_lo, preferred_element_type=F32)
              + jnp.dot(h_lo, w_hi, preferred_element_type=F32)) + br_ref[...]
    lt_ref[...] = logits.T[:N_EXPERTS, :]


def _cross(x1, cross_g, w_cq, kmem, vmem, w_co, moe_g, w_router_pad, b_router_pad, seq):
    n = x1.shape[0]
    tm = CROSS_TM
    per_seq = seq // tm
    mlen = kmem.shape[1]
    return pl.pallas_call(
        _cross_kernel,
        grid=(n // tm,),
        in_specs=[
            pl.BlockSpec((tm, D_MODEL), lambda i: (i, 0)),
            _resident((1, D_MODEL)),
            _resident(w_cq.shape),
            pl.BlockSpec((1, mlen, MEM_WIDTH), lambda i: (i // per_seq, 0, 0)),
            pl.BlockSpec((1, mlen, MEM_WIDTH), lambda i: (i // per_seq, 0, 0)),
            _resident(w_co.shape),
            _resident((1, D_MODEL)),
            _resident(w_router_pad.shape),
            _resident(b_router_pad.shape),
        ],
        out_specs=[
            pl.BlockSpec((tm, D_MODEL), lambda i: (i, 0)),
            pl.BlockSpec((tm, D_MODEL // 2), lambda i: (i, 0)),
            pl.BlockSpec((N_EXPERTS, tm), lambda i: (0, i)),
        ],
        out_shape=[
            jax.ShapeDtypeStruct((n, D_MODEL), F32),
            jax.ShapeDtypeStruct((n, D_MODEL // 2), jnp.uint32),
            jax.ShapeDtypeStruct((N_EXPERTS, n), F32),
        ],
        compiler_params=_cparams(("parallel",)),
        name="cross",
    )(x1, cross_g, w_cq, kmem, vmem, w_co, moe_g, w_router_pad, b_router_pad)


def _route_kernel(lt_ref, pos_ref, gate_ref, cnt_ref, nblk_ref, start_ref, m_ref, rank_ref):
    n = lt_ref.shape[1]
    l = lt_ref[...]
    row = lax.broadcasted_iota(jnp.int32, l.shape, 0)
    sels, vals = [], []
    for _ in range(TOP_K):
        m = jnp.max(l, axis=0, keepdims=True)
        idx = jnp.min(jnp.where(l == m, row, N_EXPERTS), axis=0, keepdims=True)
        sel = row == idx
        sels.append(sel)
        vals.append(m)
        l = jnp.where(sel, -jnp.inf, l)

    es = [jnp.exp(v - vals[0]) for v in vals]
    den = es[0] + es[1] + es[2] + es[3]
    for k in range(TOP_K):
        gate_ref[k:k + 1, :] = es[k] / den

    chosen = sels[0] | sels[1] | sels[2] | sels[3]
    m_ref[...] = jnp.where(chosen, 1.0, 0.0)

    ci = lax.broadcasted_iota(jnp.int32, (ROUTE_CHUNK, ROUTE_CHUNK), 0)
    cj = lax.broadcasted_iota(jnp.int32, (ROUTE_CHUNK, ROUTE_CHUNK), 1)
    upper = jnp.where(ci < cj, 1.0, 0.0).astype(BF16)

    def chunk(c, carry):
        cols = pl.ds(pl.multiple_of(c * ROUTE_CHUNK, ROUTE_CHUNK), ROUTE_CHUNK)
        mt = m_ref[:, cols]
        rank_ref[:, cols] = jnp.dot(mt.astype(BF16), upper, preferred_element_type=F32) + carry
        return carry + jnp.sum(mt, axis=1, keepdims=True)

    counts = lax.fori_loop(0, n // ROUTE_CHUNK, chunk, jnp.zeros((N_EXPERTS, 1), F32))
    nblk = jnp.floor((counts + (SLOT_BLK - 1)) * (1.0 / SLOT_BLK))
    ei = lax.broadcasted_iota(jnp.int32, (N_EXPERTS, N_EXPERTS), 0)
    ej = lax.broadcasted_iota(jnp.int32, (N_EXPERTS, N_EXPERTS), 1)
    lower = jnp.where(ej < ei, 1.0, 0.0).astype(BF16)
    nblk_b = jnp.broadcast_to(nblk, (N_EXPERTS, LANES))
    start_b = jnp.dot(lower, nblk_b.astype(BF16), preferred_element_type=F32)
    cnt_ref[...] = jnp.broadcast_to(counts, (N_EXPERTS, LANES)).astype(jnp.int32)
    nblk_ref[...] = nblk_b.astype(jnp.int32)
    start_ref[...] = start_b.astype(jnp.int32)

    slot = start_b[:, :1] * SLOT_BLK + rank_ref[...]
    for k in range(TOP_K):
        pos_ref[k:k + 1, :] = jnp.sum(jnp.where(sels[k], slot, 0.0), axis=0, keepdims=True).astype(jnp.int32)


def _route(logits_t):
    e, n = logits_t.shape
    small = jax.ShapeDtypeStruct((e, LANES), jnp.int32)
    return pl.pallas_call(
        _route_kernel,
        out_shape=[
            jax.ShapeDtypeStruct((TOP_K, n), jnp.int32),
            jax.ShapeDtypeStruct((TOP_K, n), F32),
            small, small, small,
        ],
        scratch_shapes=[pltpu.VMEM((e, n), F32), pltpu.VMEM((e, n), F32)],
        compiler_params=pltpu.CompilerParams(vmem_limit_bytes=VMEM_LIMIT),
        name="route",
    )(logits_t)


def _dispatch_kernel(pos_ref, cnt_ref, nblk_ref, start_ref, h_ref, xs_ref, zero_ref, sem, pad_sem, *, tm):
    i = pl.program_id(0)

    def tok(r, _):
        t = i * tm + r
        for k in range(TOP_K):
            p = pos_ref[t * TOP_K + k]
            pltpu.make_async_copy(h_ref.at[pl.ds(r, 1)], xs_ref.at[pl.ds(p, 1)], sem).start(priority=k % 2)
        return 0

    lax.fori_loop(0, tm, tok, 0)
    for _ in range(TOP_K):
        pltpu.make_async_copy(h_ref, xs_ref.at[pl.ds(0, tm)], sem).wait()

    @pl.when(i == pl.num_programs(0) - 1)
    def _():
        _dispatch_fill(cnt_ref, nblk_ref, start_ref, xs_ref, zero_ref, pad_sem)


def _dispatch_fill(cnt_ref, nblk_ref, start_ref, xs_ref, zero_ref, pad_sem):
    zero_ref[...] = jnp.zeros(zero_ref.shape, zero_ref.dtype)

    def pad_expert(e, _):
        lo = start_ref[e] * SLOT_BLK + cnt_ref[e]
        hi = (start_ref[e] + nblk_ref[e]) * SLOT_BLK

        def fill(r, _):
            pltpu.make_async_copy(zero_ref.at[pl.ds(0, 1)], xs_ref.at[pl.ds(r, 1)], pad_sem).start()
            return 0

        def drain(r, _):
            pltpu.make_async_copy(zero_ref.at[pl.ds(0, 1)], xs_ref.at[pl.ds(r, 1)], pad_sem).wait()
            return 0

        lax.fori_loop(lo, hi, fill, 0)
        lax.fori_loop(lo, hi, drain, 0)
        return 0

    lax.fori_loop(0, N_EXPERTS, pad_expert, 0)

    def tail(b, _):
        rows = pl.ds(pl.multiple_of(b * SLOT_BLK, SLOT_BLK), SLOT_BLK)
        cp = pltpu.make_async_copy(zero_ref, xs_ref.at[rows], pad_sem)
        cp.start()
        cp.wait()
        return 0

    lax.fori_loop(start_ref[N_EXPERTS - 1] + nblk_ref[N_EXPERTS - 1], xs_ref.shape[0] // SLOT_BLK, tail, 0)


def _dispatch(pos_flat, counts, nblk, start, h_packed, n_slots):
    n_tok, width = h_packed.shape
    tm = DISPATCH_CHUNK
    return pl.pallas_call(
        functools.partial(_dispatch_kernel, tm=tm),
        grid_spec=pltpu.PrefetchScalarGridSpec(
            num_scalar_prefetch=4,
            grid=(n_tok // tm,),
            in_specs=[pl.BlockSpec((tm, width), lambda i, *_: (i, 0))],
            out_specs=pl.BlockSpec(memory_space=pl.ANY),
            scratch_shapes=[
                pltpu.VMEM((SLOT_BLK, width), h_packed.dtype),
                pltpu.SemaphoreType.DMA(()),
                pltpu.SemaphoreType.DMA(()),
            ],
        ),
        out_shape=jax.ShapeDtypeStruct((n_slots, width), h_packed.dtype),
        compiler_params=pltpu.CompilerParams(dimension_semantics=("arbitrary",)),
        name="dispatch",
    )(pos_flat, counts, nblk, start, h_packed)


def _swiglu_pairs(h_a, h_b):
    lane = lax.broadcasted_iota(jnp.int32, h_a.shape, 1)
    even = (lane % 2) == 0
    gate_in = jnp.where(even, h_a, pltpu.roll(h_b, 1, 1))
    lin_in = jnp.where(even, pltpu.roll(h_a, LANES - 1, 1), h_b)
    glu = jnp.minimum(gate_in, SWIGLU_LIMIT)
    lin = jnp.clip(lin_in, -SWIGLU_LIMIT, SWIGLU_LIMIT)
    return glu * (1.0 / (1.0 + jnp.exp(-SWIGLU_ALPHA * glu))) * (lin + 1.0)


def _experts_kernel(ie_ref, ib_ref, ins_ref, ni_ref, xs_ref, w1_hbm, b1_ref, w2_hbm, b2_ref, ys_ref,
                    xraw_ref, xb_ref, w1buf_ref, w2buf_ref, w2q_ref, acc_ref, sem, xsem, wsem):
    w = pl.program_id(0)
    j = pl.program_id(1)
    n_chunks = pl.num_programs(1)
    last_j = n_chunks - 1
    n_items = ni_ref[0]
    nsub = ins_ref[w]
    slot = w % 2
    half = D_MODEL // 2
    quarter = FFN_CHUNK // 4
    g = w * n_chunks + j
    wslot = g % WEIGHT_RING

    def sub_rows(s):
        return pl.ds(s * SLOT_BLK, SLOT_BLK)

    def slot_rows(item, s):
        return pl.ds(pl.multiple_of((ib_ref[item] + s) * SLOT_BLK, SLOT_BLK), SLOT_BLK)

    def writeback(item, item_slot, s):
        return pltpu.make_async_copy(acc_ref.at[item_slot, sub_rows(s)], ys_ref.at[slot_rows(item, s)],
                                     sem.at[item_slot])

    def wait_item(item, item_slot):
        for s in range(SUB_PER_ITEM):
            @pl.when(s < ins_ref[item])
            def _(s=s):
                writeback(item, item_slot, s).wait()

    def x_copy(item, s):
        return pltpu.make_async_copy(xs_ref.at[slot_rows(item, s)], xraw_ref.at[sub_rows(s)], xsem)

    def x_copies(item, fn):
        for s in range(SUB_PER_ITEM):
            @pl.when(s < ins_ref[item])
            def _(s=s):
                fn(x_copy(item, s))

    def weight_copies(chunk, ring_slot):
        it = chunk // n_chunks
        jj = chunk % n_chunks
        e = ie_ref[it]
        cols = pl.ds(pl.multiple_of(jj * (2 * FFN_CHUNK), 2 * FFN_CHUNK), 2 * FFN_CHUNK)
        rows = pl.ds(pl.multiple_of(jj * FFN_CHUNK, FFN_CHUNK), FFN_CHUNK)
        return (pltpu.make_async_copy(w1_hbm.at[e, :, cols], w1buf_ref.at[ring_slot], wsem.at[ring_slot]),
                pltpu.make_async_copy(w2_hbm.at[e, rows, :], w2buf_ref.at[ring_slot], wsem.at[ring_slot]))

    def ffn_chunk(n):
        rows = n * SLOT_BLK
        w1 = w1buf_ref[wslot].astype(BF16)
        for dst, src in ((0, 0), (1, 2 * quarter), (2 * quarter, quarter), (2 * quarter + 1, 3 * quarter)):
            for c in range(D_MODEL // LANES):
                w2q_ref[c, pl.ds(dst, quarter, stride=2), :] = (
                    w2buf_ref[wslot, pl.ds(src, quarter), c * LANES:(c + 1) * LANES])
        w2 = jnp.concatenate([w2q_ref[c] for c in range(D_MODEL // LANES)], axis=1).astype(BF16)
        h = jnp.dot(xb_ref[0:rows, :], w1, preferred_element_type=F32) + b1_ref[...]
        act = jnp.concatenate(
            [_swiglu_pairs(h[:, 0:LANES], h[:, 2 * LANES:3 * LANES]),
             _swiglu_pairs(h[:, LANES:2 * LANES], h[:, 3 * LANES:4 * LANES])], axis=1)
        acc_ref[slot, 0:rows, :] += jnp.dot(act.astype(BF16), w2, preferred_element_type=F32)

    @pl.when(w < n_items)
    def _():
        @pl.when(g == 0)
        def _():
            x_copies(0, lambda cp: cp.start())
            for k in range(WEIGHT_RING - 1):
                for cp in weight_copies(k, k):
                    cp.start()

        ahead = g + (WEIGHT_RING - 1)

        @pl.when(ahead < n_items * n_chunks)
        def _():
            for cp in weight_copies(ahead, ahead % WEIGHT_RING):
                cp.start()

        @pl.when(j == 0)
        def _():
            @pl.when(w >= 2)
            def _():
                wait_item(w - 2, slot)

            x_copies(w, lambda cp: cp.wait())
            for s in range(SUB_PER_ITEM):
                @pl.when(s < nsub)
                def _(s=s):
                    word = xraw_ref[sub_rows(s), :]
                    xb_ref[sub_rows(s), :half] = pltpu.bitcast(word << 16, F32).astype(BF16)
                    xb_ref[sub_rows(s), half:] = pltpu.bitcast(word & jnp.uint32(0xFFFF0000), F32).astype(BF16)
                    acc_ref[slot, sub_rows(s), :] = jnp.broadcast_to(b2_ref[...], (SLOT_BLK, D_MODEL))

        @pl.when((j == 1) & (w + 1 < n_items))
        def _():
            x_copies(w + 1, lambda cp: cp.start())

        for cp in weight_copies(g, wslot):
            cp.wait()

        for n in range(1, SUB_PER_ITEM + 1):
            @pl.when(nsub == n)
            def _(n=n):
                ffn_chunk(n)

        @pl.when(j == last_j)
        def _():
            for s in range(SUB_PER_ITEM):
                @pl.when(s < nsub)
                def _(s=s):
                    writeback(w, slot, s).start()

        @pl.when((j == last_j) & (w == n_items - 1))
        def _():
            wait_item(w, slot)

            @pl.when(w >= 1)
            def _():
                wait_item(w - 1, 1 - slot)

            zero_blk = acc_ref.at[slot, sub_rows(0)]
            zero_blk[...] = jnp.zeros(zero_blk.shape, F32)

            def tail(b, _):
                rows = pl.ds(pl.multiple_of(b * SLOT_BLK, SLOT_BLK), SLOT_BLK)
                cp = pltpu.make_async_copy(zero_blk, ys_ref.at[rows], sem.at[slot])
                cp.start()
                cp.wait()
                return 0

            lax.fori_loop(ib_ref[w] + nsub, ys_ref.shape[0] // SLOT_BLK, tail, 0)


def _experts(item_e, item_b, item_n, n_items, xs, w_gate_up, b_gate_up, w_down, b_down, n_blocks, max_items):
    width = xs.shape[1]
    n_chunks = D_EXPERT // FFN_CHUNK
    item_rows = SUB_PER_ITEM * SLOT_BLK
    in_specs = [
        pl.BlockSpec(memory_space=pl.ANY),
        pl.BlockSpec(memory_space=pl.ANY),
        pl.BlockSpec((None, 1, 2 * FFN_CHUNK), lambda w, j, ie, ib, ins, ni: (ie[w], 0, j)),
        pl.BlockSpec(memory_space=pl.ANY),
        pl.BlockSpec((None, 1, D_MODEL), lambda w, j, ie, ib, ins, ni: (ie[w], 0, 0)),
    ]
    return pl.pallas_call(
        _experts_kernel,
        grid_spec=pltpu.PrefetchScalarGridSpec(
            num_scalar_prefetch=4,
            grid=(max_items, n_chunks),
            in_specs=in_specs,
            out_specs=pl.BlockSpec(memory_space=pl.ANY),
            scratch_shapes=[
                pltpu.VMEM((item_rows, width), xs.dtype),
                pltpu.VMEM((item_rows, D_MODEL), BF16),
                pltpu.VMEM((WEIGHT_RING, D_MODEL, 2 * FFN_CHUNK), F32),
                pltpu.VMEM((WEIGHT_RING, FFN_CHUNK, D_MODEL), F32),
                pltpu.VMEM((D_MODEL // LANES, FFN_CHUNK, LANES), F32),
                pltpu.VMEM((2, item_rows, D_MODEL), F32),
                pltpu.SemaphoreType.DMA((2,)),
                pltpu.SemaphoreType.DMA(()),
                pltpu.SemaphoreType.DMA((WEIGHT_RING,)),
            ],
        ),
        out_shape=jax.ShapeDtypeStruct((n_blocks * SLOT_BLK, D_MODEL), F32),
        compiler_params=_cparams(("arbitrary", "arbitrary"), EXPERTS_VMEM_LIMIT),
        name="experts",
    )(item_e, item_b, item_n, n_items, xs, w_gate_up, b_gate_up, w_down, b_down)


def _combine_kernel(pos_ref, ys_ref, gate_ref, x_ref, g_ref, o_ref, buf_ref, sem, *, tm):
    i = pl.program_id(0)
    n_steps = pl.num_programs(0)

    def issue(step, slot):
        def tok(r, _):
            t = step * tm + r
            for k in range(TOP_K):
                p = pos_ref[t * TOP_K + k]
                pltpu.make_async_copy(ys_ref.at[pl.ds(p, 1)], buf_ref.at[slot, k, pl.ds(r, 1)],
                                      sem.at[slot]).start(priority=k % 2)
            return 0
        lax.fori_loop(0, tm, tok, 0)

    @pl.when(i == 0)
    def _():
        issue(0, 0)

    @pl.when(i + 1 < n_steps)
    def _():
        issue(i + 1, (i + 1) % 2)

    slot = i % 2
    for k in range(TOP_K):
        pltpu.make_async_copy(ys_ref.at[pl.ds(0, tm)], buf_ref.at[slot, k], sem.at[slot]).wait()
    gates = gate_ref[...]
    acc = x_ref[...]
    for k in range(TOP_K):
        acc = acc + gates[:, k:k + 1] * buf_ref[slot, k]
    o_ref[...] = _rms(acc, g_ref[...])


def _combine(pos_flat, ys, gates, x2, final_g):
    n = x2.shape[0]
    tm = COMBINE_TM
    return pl.pallas_call(
        functools.partial(_combine_kernel, tm=tm),
        grid_spec=pltpu.PrefetchScalarGridSpec(
            num_scalar_prefetch=1,
            grid=(n // tm,),
            in_specs=[
                pl.BlockSpec(memory_space=pl.ANY),
                pl.BlockSpec((tm, TOP_K), lambda i, pos: (i, 0)),
                pl.BlockSpec((tm, D_MODEL), lambda i, pos: (i, 0)),
                pl.BlockSpec((1, D_MODEL), lambda i, pos: (0, 0)),
            ],
            out_specs=pl.BlockSpec((tm, D_MODEL), lambda i, pos: (i, 0)),
            scratch_shapes=[
                pltpu.VMEM((2, TOP_K, tm, D_MODEL), F32),
                pltpu.SemaphoreType.DMA((2,)),
            ],
        ),
        out_shape=jax.ShapeDtypeStruct((n, D_MODEL), F32),
        compiler_params=_cparams(("arbitrary",)),
        name="combine",
    )(pos_flat, ys, gates, x2, final_g)


def _rope_tables(seq):
    t = jnp.arange(seq, dtype=jnp.int32)
    rows = (t // GRID_W).astype(F32)
    cols = (t % GRID_W).astype(F32)
    axis_dim = HEAD_DIM // 2
    inv_freq = ROPE_THETA ** (-jnp.arange(0, axis_dim, 2, dtype=F32) / axis_dim)
    ar = rows[:, None] * inv_freq
    ac = cols[:, None] * inv_freq
    cos = jnp.concatenate([jnp.cos(ar), jnp.cos(ar), jnp.cos(ac), jnp.cos(ac)], axis=1)
    sin = jnp.concatenate([-jnp.sin(ar), jnp.sin(ar), -jnp.sin(ac), jnp.sin(ac)], axis=1)
    return cos, sin


def _work_items(nblk, start, max_items):
    per_e = (nblk + SUB_PER_ITEM - 1) // SUB_PER_ITEM
    ends = jnp.cumsum(per_e)
    n_items = ends[-1]
    w = jnp.minimum(jnp.arange(max_items, dtype=jnp.int32), n_items - 1)
    e = jnp.minimum(jnp.searchsorted(ends, w, side="right"), N_EXPERTS - 1).astype(jnp.int32)
    local = w - (ends - per_e)[e]
    blk0 = start[e] + local * SUB_PER_ITEM
    nsub = jnp.clip(nblk[e] - local * SUB_PER_ITEM, 1, SUB_PER_ITEM)
    return e, blk0.astype(jnp.int32), nsub.astype(jnp.int32), n_items.reshape(1).astype(jnp.int32)


def _layer(x2, mem, p, batch, seq, cos, sin_signed):
    n = x2.shape[0]
    w_in = p["w_in"]
    w_in_r = jnp.concatenate(
        [w_in[:, P_END:GA_END], w_in[:, GA_END:], w_in[:, V_END:P_END], w_in[:, :V_END]], axis=1).astype(BF16)
    proj = _proj(x2, p["mix_norm_g"][None, :], w_in_r)
    attn = _attention(proj, cos, sin_signed, p["q_norm_g"][None, :], p["k_norm_g"][None, :], batch, seq)
    x1 = _mix(attn, proj, x2, p["pool_w"].astype(BF16), p["pool_scale"][None, :],
              p["w_attn_up"].astype(BF16), p["w_pool_up"].astype(BF16), p["w_mix_out"].astype(BF16), seq)

    kmem, vmem = _memkv(mem, p["mem_norm_g"][None, :], p["w_ck"].astype(BF16), p["w_cv"].astype(BF16))
    w_router_pad = jnp.pad(p["w_router"], ((0, 0), (0, LANES - N_EXPERTS)))
    b_router_pad = jnp.pad(p["b_router"], (0, LANES - N_EXPERTS))[None, :]
    x2b, h_packed, logits_t = _cross(x1, p["cross_norm_g"][None, :], p["w_cq"].astype(BF16), kmem, vmem,
                                     p["w_co"].astype(BF16), p["moe_norm_g"][None, :], w_router_pad,
                                     b_router_pad, seq)

    pos, gates, counts, nblk, start = _route(logits_t)
    counts, nblk, start = counts[:, 0], nblk[:, 0], start[:, 0]
    pos_flat = pos.T.reshape(-1)
    n_blocks = (n * TOP_K) // SLOT_BLK + N_EXPERTS
    max_items = (n * TOP_K) // (SLOT_BLK * SUB_PER_ITEM) + N_EXPERTS
    xs = _dispatch(pos_flat, counts, nblk, start, h_packed, n_blocks * SLOT_BLK)
    item_e, item_b, item_n, n_items = _work_items(nblk, start, max_items)
    ys = _experts(item_e, item_b, item_n, n_items, xs, p["w_gate_up"], p["b_gate_up"][:, None, :],
                  p["w_down"], p["b_down"][:, None, :], n_blocks, max_items)
    return pos_flat, ys, gates.T, x2b


def kernel(x, mem, mix_norm_g, w_in, q_norm_g, k_norm_g, pool_w, pool_scale, w_attn_up, w_pool_up, w_mix_out,
           cross_norm_g, mem_norm_g, w_cq, w_ck, w_cv, w_co, moe_norm_g, w_router, b_router, w_gate_up,
           b_gate_up, w_down, b_down, final_norm_g):
    batch, seq, d = x.shape
    depth = w_in.shape[0]
    assert depth == 1, "the combine stage applies the final norm, so exactly one layer is supported"
    cos, sin_signed = _rope_tables(seq)
    x2 = x.reshape(batch * seq, d)
    p = dict(mix_norm_g=mix_norm_g[0], w_in=w_in[0], q_norm_g=q_norm_g[0], k_norm_g=k_norm_g[0],
             pool_w=pool_w[0], pool_scale=pool_scale[0], w_attn_up=w_attn_up[0], w_pool_up=w_pool_up[0],
             w_mix_out=w_mix_out[0], cross_norm_g=cross_norm_g[0], mem_norm_g=mem_norm_g[0], w_cq=w_cq[0],
             w_ck=w_ck[0], w_cv=w_cv[0], w_co=w_co[0], moe_norm_g=moe_norm_g[0], w_router=w_router[0],
             b_router=b_router[0], w_gate_up=w_gate_up[0], b_gate_up=b_gate_up[0], w_down=w_down[0],
             b_down=b_down[0])
    pos_flat, ys, gates, x2b = _layer(x2, mem, p, batch, seq, cos, sin_signed)
    out = _combine(pos_flat, ys, gates, x2b, final_norm_g[None, :])
    return out.reshape(batch, seq, d)
```
